```python
import math
import jax, jax.numpy as jnp
from jax import lax
import numpy as np


D_MODEL = 1024
BATCH = 2
SEQ = 8192
DEPTH = 1

NORM_EPS = 1e-6
L2_EPS = 1e-6
ROPE_THETA = 500000.0
NEG_INF = -1e30

GDN_HEADS = 8
GDN_DK = 128
GDN_DV = 128
GDN_CONV = 4
GDN_CHUNK = 64

NSA_HEADS = 8
NSA_GROUPS = 2
NSA_REP = NSA_HEADS // NSA_GROUPS
NSA_DK = 128
NSA_DV = 128
ROT_DIM = NSA_DK // 4
CMP_LEN = 32
CMP_STRIDE = 16
SEL_LEN = 64
SEL_TOPK = 16
WINDOW = 512
Q_BLOCK = 128
SEL_FORCE = 1e4

MEM_LEN = 256
XATTN_HEADS = 4
XATTN_DH = 128

N_EXPERTS = 32
TOP_K = 4
D_EXPERT = D_MODEL
SWIGLU_LIMIT = 7.0
SWIGLU_ALPHA = 1.702
MOE_BLOCK = 256

IN_WIDTHS = (
    GDN_HEADS * GDN_DK,
    GDN_HEADS * GDN_DK,
    GDN_HEADS * GDN_DV,
    GDN_HEADS * GDN_DV,
    GDN_HEADS,
    GDN_HEADS,
    NSA_HEADS * NSA_DK,
    NSA_GROUPS * NSA_DK,
    NSA_GROUPS * NSA_DV,
    NSA_GROUPS * NSA_DK,
    NSA_GROUPS * NSA_DV,
    NSA_GROUPS * NSA_DK,
    NSA_GROUPS * NSA_DV,
    NSA_HEADS * 3,
    D_MODEL,
    D_MODEL,
)
D_IN = sum(IN_WIDTHS)

kernel_name = 'hybrid_gdn_nsa_moe_block'


def rms_norm(x, gain):
    xf = x.astype(jnp.float32)
    y = xf * lax.rsqrt(jnp.mean(xf * xf, axis=-1, keepdims=True) + NORM_EPS)
    return (y * gain.astype(jnp.float32)).astype(x.dtype)


def l2_normalize(x):
    xf = x.astype(jnp.float32)
    return xf * lax.rsqrt(jnp.sum(xf * xf, axis=-1, keepdims=True) + L2_EPS)


def partial_rope(x, pos):
    half = ROT_DIM // 2
    inv_freq = jnp.exp(-math.log(ROPE_THETA) * jnp.arange(half, dtype=jnp.float32) * (2.0 / ROT_DIM))
    ang = pos.astype(jnp.float32)[:, None] * inv_freq[None, :]
    cos = jnp.cos(ang)[None, :, None, :]
    sin = jnp.sin(ang)[None, :, None, :]
    xr = x[..., :ROT_DIM].astype(jnp.float32)
    x1, x2 = xr[..., :half], xr[..., half:]
    rot = jnp.concatenate([x1 * cos - x2 * sin, x2 * cos + x1 * sin], axis=-1)
    return jnp.concatenate([rot.astype(x.dtype), x[..., ROT_DIM:]], axis=-1)


def masked_softmax(s, mask):
    s = jnp.where(mask, s.astype(jnp.float32), NEG_INF)
    m = jnp.max(s, axis=-1, keepdims=True)
    p = jnp.where(mask, jnp.exp(s - m), 0.0)
    return p / jnp.maximum(jnp.sum(p, axis=-1, keepdims=True), 1e-30)


def causal_depthwise_conv_silu(x, w):
    c = x.shape[-1]
    y = lax.conv_general_dilated(
        x, w.astype(x.dtype)[:, None, :], window_strides=(1,),
        padding=[(w.shape[0] - 1, 0)], dimension_numbers=('NWC', 'WIO', 'NWC'),
        feature_group_count=c)
    return jax.nn.silu(y)


def gated_delta_rule(q, k, v, g, beta):
    bsz, seq, nh, dk = q.shape
    dv = v.shape[-1]
    c = GDN_CHUNK
    n = seq // c
    f32 = jnp.float32

    def chunks(a):
        a = jnp.moveaxis(a.astype(f32), 2, 1)
        return a.reshape((bsz, nh, n, c) + a.shape[3:])

    q = chunks(q) * (dk ** -0.5)
    k = chunks(k)
    v = chunks(v)
    beta = chunks(beta)
    g = jnp.cumsum(chunks(g), axis=-1)
    kb = k * beta[..., None]
    vb = v * beta[..., None]
    incl = jnp.tril(jnp.ones((c, c), dtype=bool))
    strict = jnp.tril(jnp.ones((c, c), dtype=bool), -1)
    diff = g[..., :, None] - g[..., None, :]
    decay = jnp.where(incl, jnp.exp(jnp.where(incl, diff, 0.0)), 0.0)
    a_mat = jnp.where(strict, jnp.einsum('bhnid,bhnjd->bhnij', kb, k) * decay, 0.0)
    eye = jnp.eye(c, dtype=f32)
    t_mat = lax.linalg.triangular_solve(a_mat + eye, jnp.broadcast_to(eye, a_mat.shape),
                                        left_side=True, lower=True, unit_diagonal=True)
    u = t_mat @ vb
    w = t_mat @ (kb * jnp.exp(g)[..., None])
    qk = jnp.einsum('bhnid,bhnjd->bhnij', q, k) * decay
    q_dec = q * jnp.exp(g)[..., None]
    k_dec = k * jnp.exp(g[..., -1:] - g)[..., None]
    g_last = jnp.exp(g[..., -1])

    def step(state, xs):
        u_n, w_n, qk_n, qd_n, kd_n, gl_n = xs
        v_new = u_n - jnp.einsum('bhck,bhkv->bhcv', w_n, state)
        o = jnp.einsum('bhck,bhkv->bhcv', qd_n, state) + jnp.einsum('bhij,bhjv->bhiv', qk_n, v_new)
        state = state * gl_n[..., None, None] + jnp.einsum('bhck,bhcv->bhkv', kd_n, v_new)
        return state, o

    xs = (jnp.moveaxis(u, 2, 0), jnp.moveaxis(w, 2, 0), jnp.moveaxis(qk, 2, 0),
          jnp.moveaxis(q_dec, 2, 0), jnp.moveaxis(k_dec, 2, 0), jnp.moveaxis(g_last, 2, 0))
    s0 = jnp.zeros((bsz, nh, dk, dv), f32)
    _, o = lax.scan(step, s0, xs)
    o = jnp.moveaxis(o, 0, 2).reshape(bsz, nh, seq, dv)
    return jnp.moveaxis(o, 1, 2)


def nsa_attention(q, k_cmp, v_cmp, k_slc, v_slc, k_win, v_win, branch_gates,
                  pe_k, w1_k, w2_k, pe_v, w1_v, w2_v):
    bsz, seq = q.shape[0], q.shape[1]
    f32 = jnp.float32
    pos = jnp.arange(seq)
    scale = NSA_DK ** -0.5
    q_rot = partial_rope(q, pos)
    k_slc = partial_rope(k_slc, pos)
    k_win = partial_rope(k_win, pos)

    n_cmp = (seq - CMP_LEN) // CMP_STRIDE + 1
    cmp_idx = jnp.arange(n_cmp)[:, None] * CMP_STRIDE + jnp.arange(CMP_LEN)[None, :]
    cmp_end = jnp.arange(n_cmp) * CMP_STRIDE + (CMP_LEN - 1)

    def compress(a, pe, w1, w2):
        blk = a[:, cmp_idx] + pe[:, None, :]
        blk = jnp.moveaxis(blk, 3, 2).reshape(bsz, n_cmp, NSA_GROUPS, -1)
        return jax.nn.silu(blk @ w1) @ w2

    kc = compress(k_cmp, pe_k, w1_k, w2_k)
    vc = compress(v_cmp, pe_v, w1_v, w2_v)

    n_sel = seq // SEL_LEN
    top_n = min(SEL_TOPK, n_sel)
    r_s = SEL_LEN // CMP_STRIDE
    r_c = CMP_LEN // CMP_STRIDE
    offs = np.array([m - nn for m in range(r_s) for nn in range(r_c)], dtype=np.int32)
    cidx = jnp.arange(n_sel)[:, None] * r_s + offs[None, :]
    ok = ((cidx >= 0) & (cidx < n_cmp)).astype(f32)
    wmap = jnp.sum(jax.nn.one_hot(jnp.clip(cidx, 0, n_cmp - 1), n_cmp, dtype=f32) * ok[..., None], axis=1).T
    sel_ids = jnp.arange(n_sel)
    sel_start = sel_ids * SEL_LEN

    qg = q.reshape(bsz, seq, NSA_GROUPS, NSA_REP, NSA_DK)
    qrg = q_rot.reshape(bsz, seq, NSA_GROUPS, NSA_REP, NSA_DK)
    ks_t = jnp.moveaxis(k_slc, 2, 1)
    vs_t = jnp.moveaxis(v_slc, 2, 1)
    kw_pad = jnp.pad(k_win, ((0, 0), (WINDOW, 0), (0, 0), (0, 0)))
    vw_pad = jnp.pad(v_win, ((0, 0), (WINDOW, 0), (0, 0), (0, 0)))
    b_idx = jnp.arange(bsz)[:, None, None]
    g_idx = jnp.arange(NSA_GROUPS)[None, :, None]
    sel_off = jnp.arange(SEL_LEN)
    win_off = jnp.arange(WINDOW + Q_BLOCK) - WINDOW
    n_keys_sel = top_n * SEL_LEN

    def query_block(i):
        q0 = i * Q_BLOCK
        tq = q0 + jnp.arange(Q_BLOCK)
        qb = lax.dynamic_slice_in_dim(qg, q0, Q_BLOCK, axis=1)
        qrb = lax.dynamic_slice_in_dim(qrg, q0, Q_BLOCK, axis=1)
        s_c = jnp.einsum('bqgrd,bngd->bgrqn', qb, kc) * scale
        p_c = masked_softmax(s_c, cmp_end[None, :] <= tq[:, None])
        o_c = jnp.einsum('bgrqn,bngd->bqgrd', p_c, vc)
        imp = jnp.einsum('bgrqn,ns->bgqs', p_c, wmap)
        cur = tq // SEL_LEN
        forced = (sel_ids[None, :] == 0) | (sel_ids[None, :] == cur[:, None]) | (sel_ids[None, :] == cur[:, None] - 1)
        causal_blk = sel_start[None, :] <= tq[:, None]
        score = jnp.where(causal_blk, imp + jnp.where(forced, SEL_FORCE, 0.0), -SEL_FORCE)
        _, sel = lax.top_k(score, top_n)
        tok = (sel[..., None] * SEL_LEN + sel_off).reshape(bsz, NSA_GROUPS, Q_BLOCK * n_keys_sel)
        ks = ks_t[b_idx, g_idx, tok].reshape(bsz, NSA_GROUPS, Q_BLOCK, n_keys_sel, NSA_DK)
        vs = vs_t[b_idx, g_idx, tok].reshape(bsz, NSA_GROUPS, Q_BLOCK, n_keys_sel, NSA_DV)
        s_s = jnp.einsum('bqgrd,bgqkd->bgrqk', qrb, ks) * scale
        m_s = tok.reshape(bsz, NSA_GROUPS, Q_BLOCK, n_keys_sel) <= tq[None, None, :, None]
        p_s = masked_softmax(s_s, m_s[:, :, None])
        o_s = jnp.einsum('bgrqk,bgqkd->bqgrd', p_s, vs)
        kw = lax.dynamic_slice_in_dim(kw_pad, q0, WINDOW + Q_BLOCK, axis=1)
        vw = lax.dynamic_slice_in_dim(vw_pad, q0, WINDOW + Q_BLOCK, axis=1)
        tk = q0 + win_off
        dlt = tq[:, None] - tk[None, :]
        m_w = (dlt >= 0) & (dlt < WINDOW) & (tk[None, :] >= 0)
        s_w = jnp.einsum('bqgrd,bkgd->bgrqk', qrb, kw) * scale
        p_w = masked_softmax(s_w, m_w)
        o_w = jnp.einsum('bgrqk,bkgd->bqgrd', p_w, vw)
        return o_c, o_s, o_w

    n_qb = seq // Q_BLOCK
    o_c, o_s, o_w = lax.map(query_block, jnp.arange(n_qb))

    def unblock(o):
        return jnp.moveaxis(o, 0, 1).reshape(bsz, seq, NSA_HEADS, NSA_DV)

    gts = jax.nn.sigmoid(branch_gates.astype(f32))
    return gts[..., 0:1] * unblock(o_c) + gts[..., 1:2] * unblock(o_s) + gts[..., 2:3] * unblock(o_w)


def memory_cross_attention(h, m, w_q, w_kv, w_o):
    bsz, seq, _ = h.shape
    q = (h @ w_q).reshape(bsz, seq, XATTN_HEADS, XATTN_DH)
    k, v = jnp.split(m @ w_kv, 2, axis=-1)
    k = k.reshape(bsz, -1, XATTN_HEADS, XATTN_DH)
    v = v.reshape(bsz, -1, XATTN_HEADS, XATTN_DH)
    s = jnp.einsum('bthd,bmhd->bhtm', q, k).astype(jnp.float32) * (XATTN_DH ** -0.5)
    p = jax.nn.softmax(s, axis=-1)
    o = jnp.einsum('bhtm,bmhd->bthd', p, v).reshape(bsz, seq, XATTN_HEADS * XATTN_DH)
    return (o.astype(h.dtype) @ w_o)


def clamped_swiglu(gu):
    gate, up = jnp.split(gu, 2, axis=-1)
    gate = jnp.minimum(gate, SWIGLU_LIMIT)
    up = jnp.clip(up, -SWIGLU_LIMIT, SWIGLU_LIMIT)
    return gate * jax.nn.sigmoid(gate * SWIGLU_ALPHA) * (up + 1.0)


def moe_ffn(h, router_w, router_b, w_gu, b_gu, w_dn, b_dn):
    bsz, seq, d = h.shape
    n_tok = bsz * seq
    hf = h.reshape(n_tok, d)
    logits = (hf @ router_w + router_b).astype(jnp.float32)
    top_v, top_e = lax.top_k(logits, TOP_K)
    gates = jax.nn.softmax(top_v, axis=-1)
    n_pairs = n_tok * TOP_K
    flat_e = top_e.reshape(-1)
    flat_tok = jnp.arange(n_pairs) // TOP_K
    flat_gate = gates.reshape(-1)
    order = jnp.argsort(flat_e)
    e_sorted = flat_e[order]
    counts = jnp.bincount(flat_e, length=N_EXPERTS)
    padded = ((counts + MOE_BLOCK - 1) // MOE_BLOCK) * MOE_BLOCK
    start = jnp.cumsum(counts) - counts
    pstart = jnp.cumsum(padded) - padded
    dest = pstart[e_sorted] + jnp.arange(n_pairs) - start[e_sorted]
    n_rows = n_pairs + N_EXPERTS * MOE_BLOCK
    n_blocks = n_rows // MOE_BLOCK
    row_tok = jnp.zeros((n_rows,), jnp.int32).at[dest].set(flat_tok[order])
    row_gate = jnp.zeros((n_rows,), jnp.float32).at[dest].set(flat_gate[order])
    block_exp = jnp.minimum(
        jnp.searchsorted(jnp.cumsum(padded), jnp.arange(n_blocks) * MOE_BLOCK, side='right'),
        N_EXPERTS - 1)
    xs = hf[row_tok].reshape(n_blocks, MOE_BLOCK, d)

    def expert_block(args):
        xb, e = args
        gu = xb @ w_gu[e] + b_gu[e]
        return clamped_swiglu(gu) @ w_dn[e] + b_dn[e]

    y = lax.map(expert_block, (xs, block_exp)).reshape(n_rows, d)
    out = jax.ops.segment_sum(y.astype(jnp.float32) * row_gate[:, None], row_tok, num_segments=n_tok)
    return out.reshape(bsz, seq, d).astype(h.dtype)


def setup_inputs(seed: int = 0) -> dict:
    key = jax.random.key(seed)
    ks = jax.random.split(key, 32)
    f32 = jnp.float32
    L = DEPTH

    def nrm(k, shape, scale):
        return jax.random.normal(k, shape, f32) * scale

    def gain(k, shape):
        return 1.0 + 0.05 * jax.random.normal(k, shape, f32)

    dt = jnp.exp(jax.random.uniform(ks[6], (L, GDN_HEADS), f32, math.log(1e-3), math.log(1e-1)))
    xd = XATTN_HEADS * XATTN_DH
    return {
        'x': nrm(ks[0], (BATCH, SEQ, D_MODEL), 1.0),
        'mem': nrm(ks[1], (BATCH, MEM_LEN, D_MODEL), 1.0),
        'attn_norm_g': gain(ks[2], (L, D_MODEL)),
        'w_in': nrm(ks[3], (L, D_MODEL, D_IN), D_MODEL ** -0.5),
        'gdn_conv_w': nrm(ks[4], (L, GDN_CONV, 2 * GDN_HEADS * GDN_DK + GDN_HEADS * GDN_DV), GDN_CONV ** -0.5),
        'gdn_a_log': jnp.log(jax.random.uniform(ks[5], (L, GDN_HEADS), f32, 1.0, 16.0)),
        'gdn_dt_bias': dt + jnp.log(-jnp.expm1(-dt)),
        'gdn_norm_g': gain(ks[7], (L, GDN_DV)),
        'cmp_pe_k': nrm(ks[8], (L, CMP_LEN, NSA_DK), 0.1),
        'cmp_w1_k': nrm(ks[9], (L, CMP_LEN * NSA_DK, NSA_DK), (CMP_LEN * NSA_DK) ** -0.5),
        'cmp_w2_k': nrm(ks[10], (L, NSA_DK, NSA_DK), NSA_DK ** -0.5),
        'cmp_pe_v': nrm(ks[11], (L, CMP_LEN, NSA_DV), 0.1),
        'cmp_w1_v': nrm(ks[12], (L, CMP_LEN * NSA_DV, NSA_DV), (CMP_LEN * NSA_DV) ** -0.5),
        'cmp_w2_v': nrm(ks[13], (L, NSA_DV, NSA_DV), NSA_DV ** -0.5),
        'w_branch_a': nrm(ks[14], (L, GDN_HEADS * GDN_DV, D_MODEL), (GDN_HEADS * GDN_DV) ** -0.5),
        'w_branch_b': nrm(ks[15], (L, NSA_HEADS * NSA_DV, D_MODEL), (NSA_HEADS * NSA_DV) ** -0.5),
        'w_mix_out': nrm(ks[16], (L, D_MODEL, D_MODEL), D_MODEL ** -0.5),
        'xattn_norm_g': gain(ks[17], (L, D_MODEL)),
        'mem_norm_g': gain(ks[18], (L, D_MODEL)),
        'xattn_w_q': nrm(ks[19], (L, D_MODEL, xd), D_MODEL ** -0.5),
        'xattn_w_kv': nrm(ks[20], (L, D_MODEL, 2 * xd), D_MODEL ** -0.5),
        'xattn_w_o': nrm(ks[21], (L, xd, D_MODEL), xd ** -0.5),
        'ffn_norm_g': gain(ks[22], (L, D_MODEL)),
        'router_w': nrm(ks[23], (L, D_MODEL, N_EXPERTS), D_MODEL ** -0.5),
        'router_b': nrm(ks[24], (L, N_EXPERTS), 0.01),
        'w_gate_up': nrm(ks[25], (L, N_EXPERTS, D_MODEL, 2 * D_EXPERT), D_MODEL ** -0.5),
        'b_gate_up': nrm(ks[26], (L, N_EXPERTS, 2 * D_EXPERT), 0.01),
        'w_down': nrm(ks[27], (L, N_EXPERTS, D_EXPERT, D_MODEL), D_EXPERT ** -0.5),
        'b_down': nrm(ks[28], (L, N_EXPERTS, D_MODEL), 0.01),
        'final_norm_g': gain(ks[29], (D_MODEL,)),
    }


def reference(x, mem, attn_norm_g, w_in, gdn_conv_w, gdn_a_log, gdn_dt_bias, gdn_norm_g,
              cmp_pe_k, cmp_w1_k, cmp_w2_k, cmp_pe_v, cmp_w1_v, cmp_w2_v,
              w_branch_a, w_branch_b, w_mix_out, xattn_norm_g, mem_norm_g,
              xattn_w_q, xattn_w_kv, xattn_w_o, ffn_norm_g, router_w, router_b,
              w_gate_up, b_gate_up, w_down, b_down, final_norm_g):
    f32 = jnp.float32
    bsz, seq, _ = x.shape
    cuts = [int(c) for c in np.cumsum(IN_WIDTHS)[:-1]]
    gdn_qk = GDN_HEADS * GDN_DK
    for l in range(DEPTH):
        h = rms_norm(x, attn_norm_g[l])
        (g_q, g_k, g_v, g_z, g_b, g_a, n_q, n_kc, n_vc, n_ks, n_vs, n_kw, n_vw,
         n_gate, m_a, m_b) = jnp.split(h @ w_in[l], cuts, axis=-1)

        qkv = causal_depthwise_conv_silu(jnp.concatenate([g_q, g_k, g_v], axis=-1), gdn_conv_w[l])
        c_q, c_k, c_v = jnp.split(qkv, [gdn_qk, 2 * gdn_qk], axis=-1)
        c_q = l2_normalize(c_q.reshape(bsz, seq, GDN_HEADS, GDN_DK))
        c_k = l2_normalize(c_k.reshape(bsz, seq, GDN_HEADS, GDN_DK))
        c_v = c_v.reshape(bsz, seq, GDN_HEADS, GDN_DV)
        beta = jax.nn.sigmoid(g_b.astype(f32))
        log_decay = -jnp.exp(gdn_a_log[l].astype(f32)) * jax.nn.softplus(g_a.astype(f32) + gdn_dt_bias[l].astype(f32))
        o_a = gated_delta_rule(c_q, c_k, c_v, log_decay, beta)
        o_a = rms_norm(o_a, gdn_norm_g[l]) * jax.nn.silu(g_z.reshape(bsz, seq, GDN_HEADS, GDN_DV).astype(f32))
        y_a = o_a.reshape(bsz, seq, -1).astype(x.dtype) @ w_branch_a[l]

        o_b = nsa_attention(
            n_q.reshape(bsz, seq, NSA_HEADS, NSA_DK),
            n_kc.reshape(bsz, seq, NSA_GROUPS, NSA_DK), n_vc.reshape(bsz, seq, NSA_GROUPS, NSA_DV),
            n_ks.reshape(bsz, seq, NSA_GROUPS, NSA_DK), n_vs.reshape(bsz, seq, NSA_GROUPS, NSA_DV),
            n_kw.reshape(bsz, seq, NSA_GROUPS, NSA_DK), n_vw.reshape(bsz, seq, NSA_GROUPS, NSA_DV),
            n_gate.reshape(bsz, seq, NSA_HEADS, 3),
            cmp_pe_k[l], cmp_w1_k[l], cmp_w2_k[l], cmp_pe_v[l], cmp_w1_v[l], cmp_w2_v[l])
        y_b = o_b.reshape(bsz, seq, -1).astype(x.dtype) @ w_branch_b[l]

        mixed = jax.nn.sigmoid(m_a) * y_a + jax.nn.sigmoid(m_b) * y_b
        x = x + mixed @ w_mix_out[l]

        x = x + memory_cross_attention(rms_norm(x, xattn_norm_g[l]), rms_norm(mem, mem_norm_g[l]),
                                       xattn_w_q[l], xattn_w_kv[l], xattn_w_o[l])

        x = x + moe_ffn(rms_norm(x, ffn_norm_g[l]), router_w[l], router_b[l],
                        w_gate_up[l], b_gate_up[l], w_down[l], b_down[l])
    return rms_norm(x, final_norm_g)
```

```python
import functools
import math

import jax
import jax.numpy as jnp
import numpy as np
from jax import lax
from jax.experimental import pallas as pl
from jax.experimental.pallas import tpu as pltpu

F32 = jnp.float32
BF16 = jnp.bfloat16
HIGHEST = lax.Precision.HIGHEST

D_MODEL = 1024
NORM_EPS = 1e-6
L2_EPS = 1e-6
ROPE_THETA = 500000.0
NEG_INF = -1e30

GDN_HEADS = 8
GDN_DK = 128
GDN_DV = 128
GDN_CONV = 4
GDN_CHUNK = 64

NSA_HEADS = 8
NSA_GROUPS = 2
NSA_REP = NSA_HEADS // NSA_GROUPS
NSA_DK = 128
NSA_DV = 128
ROT_DIM = NSA_DK // 4
CMP_LEN = 32
CMP_STRIDE = 16
SEL_LEN = 64
SEL_TOPK = 16
WINDOW = 512
Q_BLOCK = 128
SEL_FORCE = 1e4

MEM_LEN = 256
XATTN_HEADS = 4
XATTN_DH = 128

N_EXPERTS = 32
TOP_K = 4
SWIGLU_LIMIT = 7.0
SWIGLU_ALPHA = 1.702
MOE_BLOCK = 256

LANES = 128
VMEM_LIMIT = 56 * 1024 * 1024
SEL_TILE = 512


def _cparams(n_axes, vmem=VMEM_LIMIT):
    return pltpu.CompilerParams(dimension_semantics=("arbitrary",) * n_axes, vmem_limit_bytes=vmem)


def _dot(a, b, precision=None):
    return jnp.dot(a, b, preferred_element_type=F32, precision=precision)


def _dot_nt(a, b, precision=None):
    return lax.dot_general(a, b, (((1,), (1,)), ((), ())), preferred_element_type=F32, precision=precision)


def _sigmoid(x):
    return 1.0 / (1.0 + jnp.exp(-x))


def _iota(shape, dim):
    return lax.broadcasted_iota(jnp.int32, shape, dim)


def _inproj_kernel(x_ref, g_ref, *refs):
    n = len(refs) // 2
    w_refs, o_refs = refs[:n], refs[n:]
    x = x_ref[...]
    ms = jnp.mean(x * x, axis=-1, keepdims=True)
    h = (x * lax.rsqrt(ms + NORM_EPS) * g_ref[...]).astype(BF16)
    for w_ref, o_ref in zip(w_refs, o_refs):
        o_ref[...] = _dot(h, w_ref[...]).astype(o_ref.dtype)


def _inproj(x2d, gain, weights, out_dtypes, tm=256):
    n, d = x2d.shape
    in_specs = [pl.BlockSpec((tm, d), lambda i: (i, 0)), pl.BlockSpec((1, d), lambda i: (0, 0))]
    in_specs += [pl.BlockSpec(w.shape, lambda i: (0, 0), pipeline_mode=pl.Buffered(1)) for w in weights]
    out_specs = [pl.BlockSpec((tm, w.shape[1]), lambda i: (i, 0)) for w in weights]
    out_shape = [jax.ShapeDtypeStruct((n, w.shape[1]), dt) for w, dt in zip(weights, out_dtypes)]
    return pl.pallas_call(
        _inproj_kernel, grid=(n // tm,), in_specs=in_specs, out_specs=out_specs, out_shape=out_shape,
        compiler_params=_cparams(1), name="inproj")(x2d, gain.reshape(1, d), *weights)


def _rope_tables(seq):
    half = ROT_DIM // 2
    inv_freq = jnp.exp(-math.log(ROPE_THETA) * jnp.arange(half, dtype=F32) * (2.0 / ROT_DIM))
    ang = jnp.arange(seq).astype(F32)[:, None] * inv_freq[None, :]
    cos, sin = jnp.cos(ang), jnp.sin(ang)
    ones = jnp.ones((seq, LANES - ROT_DIM), F32)
    zeros = jnp.zeros((seq, LANES - ROT_DIM), F32)
    zh = jnp.zeros((seq, half), F32)
    c = jnp.concatenate([cos, cos, ones], axis=1)
    sa = jnp.concatenate([zh, sin, zeros], axis=1)
    sb = jnp.concatenate([-sin, zh, zeros], axis=1)
    return c, sa, sb


def _rope(xh, c, sa, sb):
    half = ROT_DIM // 2
    return xh * c + pltpu.roll(xh, half, 1) * sa + pltpu.roll(xh, LANES - half, 1) * sb


def _nsa_prep_kernel(nq_ref, nkv_ref, c_ref, sa_ref, sb_ref, q_ref, qr_ref, kv_ref):
    c, sa, sb = c_ref[...], sa_ref[...], sb_ref[...]
    for h in range(NSA_HEADS):
        sl = slice(h * NSA_DK, (h + 1) * NSA_DK)
        xh = nq_ref[:, sl]
        q_ref[:, sl] = xh.astype(BF16)
        qr_ref[:, sl] = _rope(xh, c, sa, sb).astype(BF16)
    for j in range(4 * NSA_GROUPS):
        sl = slice(j * LANES, (j + 1) * LANES)
        xh = nkv_ref[:, sl]
        is_key = (j // NSA_GROUPS) % 2 == 0
        kv_ref[:, sl] = (_rope(xh, c, sa, sb) if is_key else xh).astype(BF16)


def _nsa_prep(nq, nkv, seq, tm=512):
    n = nq.shape[0]
    c, sa, sb = _rope_tables(seq)
    nt = seq // tm
    row = lambda w: pl.BlockSpec((tm, w), lambda i: (i, 0))
    tab = pl.BlockSpec((tm, LANES), lambda i: (i % nt, 0))
    return pl.pallas_call(
        _nsa_prep_kernel, grid=(n // tm,),
        in_specs=[row(nq.shape[1]), row(nkv.shape[1]), tab, tab, tab],
        out_specs=[row(nq.shape[1]), row(nq.shape[1]), row(nkv.shape[1])],
        out_shape=[jax.ShapeDtypeStruct(nq.shape, BF16), jax.ShapeDtypeStruct(nq.shape, BF16),
                   jax.ShapeDtypeStruct(nkv.shape, BF16)],
        compiler_params=_cparams(1), name="nsa_prep")(nq, nkv, c, sa, sb)


def _compress_kernel(x_ref, pea_ref, peb_ref, w1a_ref, w1b_ref, w2_ref, o_ref):
    x = x_ref[0]
    y0 = _dot((x + pea_ref[...]).astype(BF16), w1a_ref[...])
    y1 = _dot((x + peb_ref[...]).astype(BF16), w1b_ref[...])
    n = y1.shape[0]
    pre = y0 + pltpu.roll(y1, n - 1, 0)
    act = pre * _sigmoid(pre)
    o_ref[0] = _dot(act.astype(BF16), w2_ref[...]).astype(o_ref.dtype)


def _compress(x, pe, w1, w2):
    bsz, seq, gd = x.shape
    g = NSA_GROUPS
    d = gd // g
    half = CMP_LEN // 2
    assert CMP_STRIDE == half
    xr = x.reshape(bsz, seq // half, half * gd)
    eye = jnp.eye(g, dtype=F32)

    def expand_w1(w):
        return jnp.einsum("ldk,gh->lgdhk", w.reshape(half, d, d), eye).reshape(half * gd, gd).astype(BF16)

    w1a, w1b = expand_w1(w1[: half * d]), expand_w1(w1[half * d:])
    w2b = jnp.einsum("dk,gh->gdhk", w2, eye).reshape(gd, gd).astype(BF16)
    pea = jnp.broadcast_to(pe[:half, None, :], (half, g, d)).reshape(1, half * gd)
    peb = jnp.broadcast_to(pe[half:, None, :], (half, g, d)).reshape(1, half * gd)
    nrow = seq // half
    full = lambda a: pl.BlockSpec(a.shape, lambda b: (0,) * a.ndim)
    return pl.pallas_call(
        _compress_kernel, grid=(bsz,),
        in_specs=[pl.BlockSpec((1, nrow, half * gd), lambda b: (b, 0, 0)),
                  full(pea), full(peb), full(w1a), full(w1b), full(w2b)],
        out_specs=pl.BlockSpec((1, nrow, gd), lambda b: (b, 0, 0)),
        out_shape=jax.ShapeDtypeStruct((bsz, nrow, gd), BF16),
        compiler_params=_cparams(1), name="nsa_compress")(xr, pea, peb, w1a, w1b, w2b)


def _sel_wmap(seq):
    n_cmp = (seq - CMP_LEN) // CMP_STRIDE + 1
    n_sel = seq // SEL_LEN
    r_s = SEL_LEN // CMP_STRIDE
    r_c = CMP_LEN // CMP_STRIDE
    w = np.zeros((seq // CMP_STRIDE, n_sel), np.float32)
    for s in range(n_sel):
        for m in range(r_s):
            for nn in range(r_c):
                c = s * r_s + m - nn
                if 0 <= c < n_cmp:
                    w[c, s] += 1.0
    return w


def _nsa_attn_kernel(q_ref, qr_ref, kc_ref, vc_ref, ks_ref, vs_ref, kw_ref, vw_ref, gate_ref, wmap_ref, exp_ref,
                     o_ref, ind_ref, m_ref, l_ref, acc_ref, *, seq, gate_col0):
    g = pl.program_id(1)
    i = pl.program_id(2)
    q0 = i * Q_BLOCK
    scale = NSA_DK ** -0.5
    rows = NSA_REP * Q_BLOCK
    n_sel = seq // SEL_LEN
    n_cmp_pad = seq // CMP_STRIDE

    def heads_to_rows(ref):
        return jnp.concatenate([ref[0, :, r * NSA_DK:(r + 1) * NSA_DK] for r in range(NSA_REP)], axis=0)

    def split_heads(a):
        return a.reshape(NSA_REP, Q_BLOCK, a.shape[-1])

    qb = heads_to_rows(q_ref)
    qrb = heads_to_rows(qr_ref)
    tq1 = q0 + _iota((Q_BLOCK, 1), 0)

    s_c = split_heads(_dot_nt(qb, kc_ref[0]) * scale)
    cmp_end = _iota((1, n_cmp_pad), 1) * CMP_STRIDE + (CMP_LEN - 1)
    mask_c = (cmp_end <= tq1)[None]
    s_c = jnp.where(mask_c, s_c, NEG_INF)
    m_c = jnp.max(s_c, axis=-1, keepdims=True)
    p_c = jnp.where(mask_c, jnp.exp(s_c - m_c), 0.0)
    p_c = p_c / jnp.maximum(jnp.sum(p_c, axis=-1, keepdims=True), 1e-30)
    o_c = _dot(p_c.reshape(rows, n_cmp_pad).astype(BF16), vc_ref[0])

    p_sum = p_c[0]
    for r in range(1, NSA_REP):
        p_sum = p_sum + p_c[r]
    imp = _dot(p_sum, wmap_ref[...], precision=HIGHEST)
    sel_ids = _iota((1, n_sel), 1)
    cur = jnp.right_shift(tq1, int(math.log2(SEL_LEN)))
    forced = (sel_ids == 0) | (sel_ids == cur) | (sel_ids == cur - 1)
    causal_blk = sel_ids * SEL_LEN <= tq1
    score = jnp.where(causal_blk, imp + jnp.where(forced, SEL_FORCE, 0.0), -SEL_FORCE)
    ind = jnp.zeros((Q_BLOCK, n_sel), F32)
    lane = _iota((Q_BLOCK, n_sel), 1).astype(F32)
    for _ in range(min(SEL_TOPK, n_sel)):
        mx = jnp.max(score, axis=-1, keepdims=True)
        first = jnp.min(jnp.where(score == mx, lane, float(n_sel)), axis=-1, keepdims=True)
        pick = lane == first
        ind = jnp.where(pick, 1.0, ind)
        score = jnp.where(pick, -jnp.inf, score)
    ind_ref[...] = jnp.where(causal_blk, ind, 0.0).astype(BF16)

    m_ref[...] = jnp.full(m_ref.shape, NEG_INF, F32)
    l_ref[...] = jnp.zeros(l_ref.shape, F32)
    acc_ref[...] = jnp.zeros(acc_ref.shape, F32)
    n_tiles = (q0 + Q_BLOCK + SEL_TILE - 1) // SEL_TILE

    def sel_step(kt, carry):
        k0 = pl.multiple_of(kt * SEL_TILE, SEL_TILE)
        s = split_heads(_dot_nt(qrb, ks_ref[0, pl.ds(k0, SEL_TILE), :]) * scale)
        blk = _dot(ind_ref[...], exp_ref[kt])
        ok = ((blk > 0.5) & (k0 + _iota((1, SEL_TILE), 1) <= tq1))[None]
        s = jnp.where(ok, s, NEG_INF)
        m_old = m_ref[...]
        m_new = jnp.maximum(m_old, jnp.max(s, axis=-1, keepdims=True))
        p = jnp.where(ok, jnp.exp(s - m_new), 0.0)
        alpha = jnp.exp(m_old - m_new)
        l_ref[...] = alpha * l_ref[...] + jnp.sum(p, axis=-1, keepdims=True)
        pv = _dot(p.reshape(rows, SEL_TILE).astype(BF16), vs_ref[0, pl.ds(k0, SEL_TILE), :])
        acc_ref[...] = alpha * acc_ref[...] + split_heads(pv)
        m_ref[...] = m_new
        return carry

    lax.fori_loop(0, n_tiles, sel_step, 0)
    o_s = (acc_ref[...] / jnp.maximum(l_ref[...], 1e-30)).reshape(rows, NSA_DV)

    wlen = WINDOW + Q_BLOCK
    w0 = pl.multiple_of(jnp.maximum(q0 - WINDOW, 0), Q_BLOCK)
    s_w = split_heads(_dot_nt(qrb, kw_ref[0, pl.ds(w0, wlen), :]) * scale)
    dlt = tq1 - (w0 + _iota((1, wlen), 1))
    mask_w = ((dlt >= 0) & (dlt < WINDOW))[None]
    s_w = jnp.where(mask_w, s_w, NEG_INF)
    m_w = jnp.max(s_w, axis=-1, keepdims=True)
    p_w = jnp.where(mask_w, jnp.exp(s_w - m_w), 0.0)
    p_w = p_w / jnp.maximum(jnp.sum(p_w, axis=-1, keepdims=True), 1e-30)
    o_w = _dot(p_w.reshape(rows, wlen).astype(BF16), vw_ref[0, pl.ds(w0, wlen), :])

    gts = _sigmoid(gate_ref[0])
    for r in range(NSA_REP):
        rs = slice(r * Q_BLOCK, (r + 1) * Q_BLOCK)
        cols = []
        for j in range(3):
            col = gate_col0 + (g * NSA_REP + r) * 3 + j
            onehot = _iota((1, LANES), 1) == col
            cols.append(jnp.sum(jnp.where(onehot, gts, 0.0), axis=-1, keepdims=True))
        out = cols[0] * o_c[rs] + cols[1] * o_s[rs] + cols[2] * o_w[rs]
        o_ref[0, :, r * NSA_DV:(r + 1) * NSA_DV] = out.astype(o_ref.dtype)


def _nsa_attention(q, qr, kv, kc, vc, small, bsz, seq, gate_col0):
    g = NSA_GROUPS
    n_qb = seq // Q_BLOCK
    n_sel = seq // SEL_LEN
    wmap = jnp.asarray(_sel_wmap(seq))
    key_blk = (np.arange(seq) // SEL_LEN).reshape(seq // SEL_TILE, 1, SEL_TILE)
    expand = jnp.asarray((np.arange(n_sel)[None, :, None] == key_blk).astype(np.float32), BF16)
    rows = NSA_REP * Q_BLOCK
    qspec = pl.BlockSpec((1, Q_BLOCK, NSA_REP * NSA_DK), lambda b, gg, i: (b, i, gg))
    cspec = pl.BlockSpec((1, seq // CMP_STRIDE, LANES), lambda b, gg, i: (b, 0, gg))
    kvspec = lambda j: pl.BlockSpec((1, seq, LANES), lambda b, gg, i, j=j: (b, 0, j * g + gg))
    kernel = functools.partial(_nsa_attn_kernel, seq=seq, gate_col0=gate_col0)
    return pl.pallas_call(
        kernel, grid=(bsz, g, n_qb),
        in_specs=[qspec, qspec, cspec, cspec, kvspec(0), kvspec(1), kvspec(2), kvspec(3),
                  pl.BlockSpec((1, Q_BLOCK, LANES), lambda b, gg, i: (b, i, 0)),
                  pl.BlockSpec(wmap.shape, lambda b, gg, i: (0, 0)),
                  pl.BlockSpec(expand.shape, lambda b, gg, i: (0, 0, 0))],
        out_specs=pl.BlockSpec((1, Q_BLOCK, NSA_REP * NSA_DV), lambda b, gg, i: (b, i, gg)),
        out_shape=jax.ShapeDtypeStruct((bsz, seq, NSA_HEADS * NSA_DV), BF16),
        scratch_shapes=[pltpu.VMEM((Q_BLOCK, n_sel), BF16), pltpu.VMEM((NSA_REP, Q_BLOCK, 1), F32),
                        pltpu.VMEM((NSA_REP, Q_BLOCK, 1), F32), pltpu.VMEM((NSA_REP, Q_BLOCK, NSA_DV), F32)],
        compiler_params=_cparams(3), name="nsa_attn")(q, qr, kc, vc, kv, kv, kv, kv, small, wmap, expand)


GDN_UNIT = 2 * GDN_CHUNK
CONV_HALO = 8


def _unit_lower_inverse(a, row, col):
    eye = (row == col).astype(F32)
    blk = lambda n: jnp.right_shift(row, n) == jnp.right_shift(col, n)
    b16, b32 = blk(4), blk(5)
    hd = functools.partial(_dot, precision=HIGHEST)
    d = jnp.where(b16, a, 0.0)
    d2 = hd(d, d)
    d4 = hd(d2, d2)
    d8 = hd(d4, d4)
    t = hd(hd(hd(eye - d, eye + d2), eye + d4), eye + d8)
    off = jnp.where(b32 & ~b16, a, 0.0)
    t = t - hd(t, hd(off, t))
    off = jnp.where(~b32, a, 0.0)
    t = t - hd(t, hd(off, t))
    return t


def _gdn_kernel(x_ref, halo_ref, z_ref, small_ref, convw_ref, alog_ref, dtb_ref, ng_ref, o_ref, s_ref):
    i = pl.program_id(1)
    u_len = GDN_UNIT
    c_len = GDN_CHUNK
    hq = GDN_HEADS * GDN_DK

    @pl.when(i == 0)
    def _():
        s_ref[...] = jnp.zeros(s_ref.shape, F32)

    row = _iota((u_len, u_len), 0)
    col = _iota((u_len, u_len), 1)
    same = jnp.right_shift(row, 6) == jnp.right_shift(col, 6)
    incl = same & (row >= col)
    strict = same & (row > col)
    rcol = _iota((u_len, 1), 0)
    crow = _iota((1, u_len), 1)
    lane = _iota((1, LANES), 1)
    first_rows = rcol < c_len
    first_cols = crow < c_len
    halo_on = (i > 0).astype(F32)

    sm = small_ref[0]
    beta_all = _sigmoid(sm)
    xs = sm + dtb_ref[...]
    softplus = jnp.maximum(xs, 0.0) + jnp.log1p(jnp.exp(-jnp.abs(xs)))
    ld_all = -jnp.exp(alog_ref[...]) * softplus
    gc_all = _dot(incl.astype(F32), ld_all, precision=HIGHEST)

    def pick_col(a, c):
        return jnp.sum(jnp.where(lane == c, a, 0.0), axis=-1, keepdims=True)

    def conv_silu(c0):
        xf = jnp.concatenate([halo_ref[0, :, c0:c0 + LANES] * halo_on, x_ref[0, :, c0:c0 + LANES]], axis=0)
        w = convw_ref[:, c0:c0 + LANES]
        y = w[0:1] * xf[CONV_HALO - 3:CONV_HALO - 3 + u_len]
        for j in range(1, GDN_CONV):
            off = CONV_HALO - (GDN_CONV - 1) + j
            y = y + w[j:j + 1] * xf[off:off + u_len]
        return y * _sigmoid(y)

    for h in range(GDN_HEADS):
        q = conv_silu(h * GDN_DK)
        k = conv_silu(hq + h * GDN_DK)
        v = conv_silu(2 * hq + h * GDN_DV)
        q = q * lax.rsqrt(jnp.sum(q * q, axis=-1, keepdims=True) + L2_EPS) * (GDN_DK ** -0.5)
        k = k * lax.rsqrt(jnp.sum(k * k, axis=-1, keepdims=True) + L2_EPS)
        beta = pick_col(beta_all, h)
        gcol = pick_col(gc_all, GDN_HEADS + h)
        grow = jnp.sum(jnp.where(row == col, gcol, 0.0), axis=0, keepdims=True)
        gl0 = jnp.sum(jnp.where(rcol == c_len - 1, gcol, 0.0), axis=0, keepdims=True)
        gl1 = jnp.sum(jnp.where(rcol == u_len - 1, gcol, 0.0), axis=0, keepdims=True)
        glast_row = jnp.where(first_cols, gl0, gl1)
        e_g = jnp.exp(gcol)
        decay = jnp.where(incl, jnp.exp(jnp.where(incl, gcol - grow, 0.0)), 0.0)
        kb = k * beta
        a_mat = jnp.where(strict, _dot_nt(kb, k, precision=HIGHEST) * decay, 0.0)
        t_mat = _unit_lower_inverse(a_mat, row, col)
        u = _dot(t_mat, v * beta, precision=HIGHEST)
        w = _dot(t_mat, kb * e_g, precision=HIGHEST)
        qk = (jnp.where(incl, _dot_nt(q.astype(BF16), k.astype(BF16)) * decay, 0.0)).astype(BF16)
        q_dec = (q * e_g).astype(BF16)
        kd_t = k.T * jnp.exp(glast_row - grow)
        kd0 = jnp.where(first_cols, kd_t, 0.0).astype(BF16)
        kd1 = jnp.where(first_cols, 0.0, kd_t).astype(BF16)
        w16 = w.astype(BF16)

        s0 = s_ref[h]
        s0b = s0.astype(BF16)
        vn0 = u - _dot(w16, s0b)
        o0 = _dot(q_dec, s0b) + _dot(qk, vn0.astype(BF16))
        s1 = s0 * jnp.exp(gl0) + _dot(kd0, vn0.astype(BF16))
        s1b = s1.astype(BF16)
        vn1 = u - _dot(w16, s1b)
        o1 = _dot(q_dec, s1b) + _dot(qk, vn1.astype(BF16))
        s_ref[h] = s1 * jnp.exp(gl1) + _dot(kd1, vn1.astype(BF16))
        o = jnp.where(first_rows, o0, o1)

        o = o * lax.rsqrt(jnp.mean(o * o, axis=-1, keepdims=True) + NORM_EPS) * ng_ref[...]
        z = z_ref[0, :, h * GDN_DV:(h + 1) * GDN_DV]
        o_ref[0, :, h * GDN_DV:(h + 1) * GDN_DV] = (o * (z * _sigmoid(z))).astype(o_ref.dtype)


def _gdn(gqkv, gz, small, conv_w, a_log, dt_bias, norm_g, bsz, seq):
    u_len = GDN_UNIT
    c_all = gqkv.shape[-1]
    hv = GDN_HEADS * GDN_DV
    pad = jnp.zeros((GDN_HEADS,), F32)
    tail = jnp.zeros((LANES - 2 * GDN_HEADS,), F32)
    alog_row = jnp.concatenate([pad, a_log.astype(F32), tail]).reshape(1, LANES)
    dtb_row = jnp.concatenate([pad, dt_bias.astype(F32), tail]).reshape(1, LANES)
    per_halo = u_len // CONV_HALO
    return pl.pallas_call(
        _gdn_kernel, grid=(bsz, seq // u_len),
        in_specs=[pl.BlockSpec((1, u_len, c_all), lambda b, i: (b, i, 0)),
                  pl.BlockSpec((1, CONV_HALO, c_all), lambda b, i: (b, jnp.maximum(i * per_halo - 1, 0), 0)),
                  pl.BlockSpec((1, u_len, hv), lambda b, i: (b, i, 0)),
                  pl.BlockSpec((1, u_len, LANES), lambda b, i: (b, i, 0)),
                  pl.BlockSpec(conv_w.shape, lambda b, i: (0, 0)),
                  pl.BlockSpec((1, LANES), lambda b, i: (0, 0)),
                  pl.BlockSpec((1, LANES), lambda b, i: (0, 0)),
                  pl.BlockSpec((1, GDN_DV), lambda b, i: (0, 0))],
        out_specs=pl.BlockSpec((1, u_len, hv), lambda b, i: (b, i, 0)),
        out_shape=jax.ShapeDtypeStruct((bsz, seq, hv), BF16),
        scratch_shapes=[pltpu.VMEM((GDN_HEADS, GDN_DK, GDN_DV), F32)],
        compiler_params=_cparams(2), name="gdn")(gqkv, gqkv, gz, small, conv_w, alog_row, dtb_row,
                                                  norm_g.reshape(1, GDN_DV))


def _rms(x, gain):
    return x * lax.rsqrt(jnp.mean(x * x, axis=-1, keepdims=True) + NORM_EPS) * gain


def _mem_kv_kernel(m_ref, g_ref, w_ref, o_ref):
    h = _rms(m_ref[0], g_ref[...]).astype(BF16)
    o_ref[0] = _dot(h, w_ref[...]).astype(o_ref.dtype)


def _mem_kv(mem, gain, w_kv):
    bsz, m_len, d = mem.shape
    w = w_kv.astype(BF16)
    return pl.pallas_call(
        _mem_kv_kernel, grid=(bsz,),
        in_specs=[pl.BlockSpec((1, m_len, d), lambda b: (b, 0, 0)), pl.BlockSpec((1, d), lambda b: (0, 0)),
                  pl.BlockSpec(w.shape, lambda b: (0, 0))],
        out_specs=pl.BlockSpec((1, m_len, w.shape[1]), lambda b: (b, 0, 0)),
        out_shape=jax.ShapeDtypeStruct((bsz, m_len, w.shape[1]), BF16),
        compiler_params=_cparams(1), name="mem_kv")(mem, gain.reshape(1, d), w)


ROUTE_E, ROUTE_G, ROUTE_R = 0, TOP_K, 2 * TOP_K


def _post_mixer_kernel(oa_ref, ob_ref, mab_ref, x_ref, wa_ref, wb_ref, wmix_ref, gx_ref, wq_ref, km_ref, vm_ref,
                       wo_ref, gf_ref, rw_ref, rb_ref, x2_ref, h3_ref, route_ref, cnt_ref, run_ref):
    i = pl.program_id(0)
    tm, d = x_ref.shape

    @pl.when(i == 0)
    def _():
        run_ref[...] = jnp.zeros(run_ref.shape, F32)

    ya = _dot(oa_ref[...], wa_ref[...])
    yb = _dot(ob_ref[...], wb_ref[...])
    mixed = _sigmoid(mab_ref[:, :d]) * ya + _sigmoid(mab_ref[:, d:]) * yb
    x1 = x_ref[...] + _dot(mixed.astype(BF16), wmix_ref[...])

    hq = _rms(x1, gx_ref[...]).astype(BF16)
    q = _dot(hq, wq_ref[...])
    outs = []
    for hd in range(XATTN_HEADS):
        sl = slice(hd * XATTN_DH, (hd + 1) * XATTN_DH)
        s = _dot_nt(q[:, sl].astype(BF16), km_ref[0, :, sl]) * (XATTN_DH ** -0.5)
        p = jnp.exp(s - jnp.max(s, axis=-1, keepdims=True))
        p = p / jnp.sum(p, axis=-1, keepdims=True)
        outs.append(_dot(p.astype(BF16), vm_ref[0, :, sl]))
    o = jnp.concatenate(outs, axis=-1).astype(BF16)
    x2 = x1 + _dot(o, wo_ref[...])
    x2_ref[...] = x2

    h3 = _rms(x2, gf_ref[...])
    h3_ref[...] = h3
    logits = _dot(h3, rw_ref[...], precision=HIGHEST) + rb_ref[...]
    lane = _iota((tm, LANES), 1).astype(F32)
    work = logits
    vals, idxs = [], []
    for _ in range(TOP_K):
        mx = jnp.max(work, axis=-1, keepdims=True)
        first = jnp.min(jnp.where(work == mx, lane, float(LANES)), axis=-1, keepdims=True)
        vals.append(mx)
        idxs.append(first)
        work = jnp.where(lane == first, -jnp.inf, work)
    exps = [jnp.exp(v - vals[0]) for v in vals]
    den = exps[0]
    for e in exps[1:]:
        den = den + e
    onehot = jnp.zeros((tm, LANES), F32)
    for idx in idxs:
        onehot = onehot + (lane == idx).astype(F32)
    earlier = (_iota((tm, tm), 0) > _iota((tm, tm), 1)).astype(BF16)
    before = run_ref[...] + _dot(earlier, onehot.astype(BF16))
    route = jnp.zeros((tm, LANES), F32)
    for k in range(TOP_K):
        rank = jnp.sum(jnp.where(lane == idxs[k], before, 0.0), axis=-1, keepdims=True)
        route = jnp.where(lane == float(ROUTE_E + k), idxs[k], route)
        route = jnp.where(lane == float(ROUTE_G + k), exps[k] / den, route)
        route = jnp.where(lane == float(ROUTE_R + k), rank, route)
    route_ref[...] = route
    run_ref[...] = run_ref[...] + jnp.sum(onehot, axis=0, keepdims=True)
    cnt_ref[...] = run_ref[...]


def _post_mixer(o_a, o_b, mab, x2d, w_a, w_b, w_mix, gx, w_q, memkv, w_o, gf, router_w, router_b, seq, tm=256):
    n, d = x2d.shape
    xd = w_q.shape[1]
    m_len = memkv.shape[1]
    per_b = seq // tm
    n_exp = router_w.shape[1]
    rw = jnp.concatenate([router_w.astype(F32), jnp.zeros((d, LANES - n_exp), F32)], axis=1)
    rb = jnp.concatenate([router_b.astype(F32), jnp.full((LANES - n_exp,), NEG_INF, F32)]).reshape(1, LANES)
    row = lambda w: pl.BlockSpec((tm, w), lambda i: (i, 0))
    full = lambda a: pl.BlockSpec(a.shape, lambda i: (0,) * a.ndim)
    wa, wb, wm, wq, wo = (w.astype(BF16) for w in (w_a, w_b, w_mix, w_q, w_o))
    gx2, gf2 = gx.reshape(1, d), gf.reshape(1, d)
    return pl.pallas_call(
        _post_mixer_kernel, grid=(n // tm,),
        in_specs=[row(d), row(d), row(2 * d), row(d), full(wa), full(wb), full(wm), full(gx2), full(wq),
                  pl.BlockSpec((1, m_len, xd), lambda i: (i // per_b, 0, 0)),
                  pl.BlockSpec((1, m_len, xd), lambda i: (i // per_b, 0, 1)),
                  full(wo), full(gf2), full(rw), full(rb)],
        out_specs=[row(d), row(d), row(LANES), pl.BlockSpec((1, LANES), lambda i: (0, 0))],
        out_shape=[jax.ShapeDtypeStruct((n, d), F32), jax.ShapeDtypeStruct((n, d), F32),
                   jax.ShapeDtypeStruct((n, LANES), F32), jax.ShapeDtypeStruct((1, LANES), F32)],
        scratch_shapes=[pltpu.VMEM((1, LANES), F32)],
        compiler_params=_cparams(1), name="post_mixer")(o_a, o_b, mab, x2d, wa, wb, wm, gx2, wq, memkv, memkv, wo,
                                                        gf2, rw, rb)


MOE_TOK_TILE = 128


def _dispatch_kernel(dest_ref, h_ref, xs_in_ref, xs_ref, sem):
    del xs_in_ref
    tt = h_ref.shape[0]

    def row_copy(t, dst_row):
        return pltpu.make_async_copy(h_ref.at[pl.ds(t, 1)], xs_ref.at[pl.ds(dst_row, 1)], sem)

    def start(t, c):
        for k in range(TOP_K):
            row_copy(t, dest_ref[0, 0, t * TOP_K + k]).start()
        return c

    def wait(t, c):
        for k in range(TOP_K):
            row_copy(t, dest_ref[0, 0, t * TOP_K + k]).wait()
        return c

    lax.fori_loop(0, tt, start, 0)
    lax.fori_loop(0, tt, wait, 0)


def _dispatch(h3, dest, n_rows):
    n, d = h3.shape
    tt = MOE_TOK_TILE
    dest3 = dest.reshape(n // tt, 1, tt * TOP_K)
    xs0 = jnp.zeros((n_rows, d), h3.dtype)
    return pl.pallas_call(
        _dispatch_kernel, grid=(n // tt,),
        in_specs=[pl.BlockSpec((1, 1, tt * TOP_K), lambda i: (i, 0, 0), memory_space=pltpu.SMEM),
                  pl.BlockSpec((tt, d), lambda i: (i, 0)),
                  pl.BlockSpec(memory_space=pl.ANY)],
        out_specs=pl.BlockSpec(memory_space=pl.ANY),
        out_shape=jax.ShapeDtypeStruct((n_rows, d), h3.dtype),
        scratch_shapes=[pltpu.SemaphoreType.DMA(())],
        input_output_aliases={2: 0},
        compiler_params=_cparams(1), name="moe_dispatch")(dest3, h3, xs0)


def _expert_kernel(be_ref, nu_ref, xs_ref, wgu_ref, bgu_ref, wdn_ref, bdn_ref, y_ref, wgu_bf, wdn_bf):
    j = pl.program_id(0)
    d_exp = wdn_ref.shape[1]
    used = j < nu_ref[0]
    fresh = (j == 0) | (be_ref[j] != be_ref[jnp.maximum(j - 1, 0)])
    chunk = 128

    @pl.when(used & fresh)
    def _():
        def cast_gu(c, carry):
            r0 = pl.multiple_of(c * chunk, chunk)
            wgu_bf[pl.ds(r0, chunk), :] = wgu_ref[0, pl.ds(r0, chunk), :].astype(BF16)
            return carry

        def cast_dn(c, carry):
            r0 = pl.multiple_of(c * chunk, chunk)
            wdn_bf[pl.ds(r0, chunk), :] = wdn_ref[0, pl.ds(r0, chunk), :].astype(BF16)
            return carry

        lax.fori_loop(0, wgu_bf.shape[0] // chunk, cast_gu, 0)
        lax.fori_loop(0, wdn_bf.shape[0] // chunk, cast_dn, 0)

    @pl.when(used)
    def _():
        gu = _dot(xs_ref[...].astype(BF16), wgu_bf[...]) + bgu_ref[0]
        gate = jnp.minimum(gu[:, :d_exp], SWIGLU_LIMIT)
        up = jnp.clip(gu[:, d_exp:], -SWIGLU_LIMIT, SWIGLU_LIMIT)
        act = gate * _sigmoid(gate * SWIGLU_ALPHA) * (up + 1.0)
        y_ref[...] = _dot(act.astype(BF16), wdn_bf[...]) + bdn_ref[0]

    @pl.when(jnp.logical_not(used))
    def _():
        y_ref[...] = jnp.zeros(y_ref.shape, y_ref.dtype)


def _experts(xs, block_exp, n_used, w_gu, b_gu, w_dn, b_dn):
    n_rows, d = xs.shape
    n_exp, _, gu_w = w_gu.shape
    d_exp = w_dn.shape[1]
    n_blocks = n_rows // MOE_BLOCK
    grid_spec = pltpu.PrefetchScalarGridSpec(
        num_scalar_prefetch=2, grid=(n_blocks,),
        in_specs=[pl.BlockSpec((MOE_BLOCK, d), lambda j, be, nu: (j, 0)),
                  pl.BlockSpec((1, d, gu_w), lambda j, be, nu: (be[j], 0, 0)),
                  pl.BlockSpec((1, 1, gu_w), lambda j, be, nu: (be[j], 0, 0)),
                  pl.BlockSpec((1, d_exp, d), lambda j, be, nu: (be[j], 0, 0)),
                  pl.BlockSpec((1, 1, d), lambda j, be, nu: (be[j], 0, 0))],
        out_specs=pl.BlockSpec((MOE_BLOCK, d), lambda j, be, nu: (j, 0)),
        scratch_shapes=[pltpu.VMEM((d, gu_w), BF16), pltpu.VMEM((d_exp, d), BF16)])
    return pl.pallas_call(
        _expert_kernel, grid_spec=grid_spec,
        out_shape=jax.ShapeDtypeStruct((n_rows, d), F32),
        compiler_params=_cparams(1), name="moe_experts")(
            block_exp, n_used, xs, w_gu, b_gu.reshape(n_exp, 1, gu_w), w_dn, b_dn.reshape(n_exp, 1, d))


def _combine_kernel(dest_ref, y_ref, x_ref, route_ref, g_ref, o_ref, buf, sem, *, final_norm):
    tt = x_ref.shape[0]

    def row_copy(t, k, src_row):
        return pltpu.make_async_copy(y_ref.at[pl.ds(src_row, 1)], buf.at[k, pl.ds(t, 1)], sem)

    def start(t, c):
        for k in range(TOP_K):
            row_copy(t, k, dest_ref[0, 0, t * TOP_K + k]).start()
        return c

    def wait(t, c):
        for k in range(TOP_K):
            row_copy(t, k, dest_ref[0, 0, t * TOP_K + k]).wait()
        return c

    lax.fori_loop(0, tt, start, 0)
    lax.fori_loop(0, tt, wait, 0)
    lane = _iota((1, LANES), 1)
    rt = route_ref[...]
    moe = jnp.zeros(x_ref.shape, F32)
    for k in range(TOP_K):
        gate = jnp.sum(jnp.where(lane == ROUTE_G + k, rt, 0.0), axis=-1, keepdims=True)
        moe = moe + gate * buf[k]
    out = x_ref[...] + moe
    o_ref[...] = _rms(out, g_ref[...]) if final_norm else out


def _combine(y, dest, x2, route, final_g, final_norm):
    n, d = x2.shape
    tt = MOE_TOK_TILE
    dest3 = dest.reshape(n // tt, 1, tt * TOP_K)
    kernel = functools.partial(_combine_kernel, final_norm=final_norm)
    return pl.pallas_call(
        kernel, grid=(n // tt,),
        in_specs=[pl.BlockSpec((1, 1, tt * TOP_K), lambda i: (i, 0, 0), memory_space=pltpu.SMEM),
                  pl.BlockSpec(memory_space=pl.ANY),
                  pl.BlockSpec((tt, d), lambda i: (i, 0)),
                  pl.BlockSpec((tt, LANES), lambda i: (i, 0)),
                  pl.BlockSpec((1, d), lambda i: (0, 0))],
        out_specs=pl.BlockSpec((tt, d), lambda i: (i, 0)),
        out_shape=jax.ShapeDtypeStruct((n, d), F32),
        scratch_shapes=[pltpu.VMEM((TOP_K, tt, d), F32), pltpu.SemaphoreType.DMA(())],
        compiler_params=_cparams(1), name="moe_combine")(dest3, y, x2, route, final_g.reshape(1, d))


def _moe(h3, x2, route, cnt, w_gu, b_gu, w_dn, b_dn, final_g, final_norm):
    n, d = h3.shape
    n_exp = w_gu.shape[0]
    e = route[:, ROUTE_E:ROUTE_E + TOP_K].astype(jnp.int32)
    rank = route[:, ROUTE_R:ROUTE_R + TOP_K].astype(jnp.int32)
    counts = cnt[0, :n_exp].astype(jnp.int32)
    padded = ((counts + MOE_BLOCK - 1) // MOE_BLOCK) * MOE_BLOCK
    pend = jnp.cumsum(padded)
    dest = (pend - padded)[e] + rank
    n_rows = n * TOP_K + n_exp * MOE_BLOCK
    n_blocks = n_rows // MOE_BLOCK
    block_exp = jnp.minimum(jnp.searchsorted(pend, jnp.arange(n_blocks) * MOE_BLOCK, side="right"),
                            n_exp - 1).astype(jnp.int32)
    n_used = (pend[-1:] // MOE_BLOCK).astype(jnp.int32)
    xs = _dispatch(h3, dest, n_rows)
    y = _experts(xs, block_exp, n_used, w_gu, b_gu, w_dn, b_dn)
    return _combine(y, dest, x2, route, final_g, final_norm)


def _inproj_weights(w):
    gh = GDN_HEADS
    c = np.cumsum([0, gh * GDN_DK, gh * GDN_DK, gh * GDN_DV, gh * GDN_DV, gh, gh, NSA_HEADS * NSA_DK,
                   NSA_GROUPS * NSA_DK, NSA_GROUPS * NSA_DV, NSA_GROUPS * NSA_DK, NSA_GROUPS * NSA_DV,
                   NSA_GROUPS * NSA_DK, NSA_GROUPS * NSA_DV, NSA_HEADS * 3, D_MODEL, D_MODEL])
    n_small = 2 * gh + NSA_HEADS * 3
    small = jnp.concatenate([w[:, c[4]:c[6]], w[:, c[13]:c[14]], jnp.zeros((w.shape[0], LANES - n_small), w.dtype)],
                            axis=1)
    groups = [w[:, c[0]:c[3]],
              w[:, c[3]:c[4]],
              small,
              w[:, c[6]:c[7]],
              w[:, c[7]:c[8]],
              w[:, c[8]:c[9]],
              w[:, c[9]:c[13]],
              w[:, c[14]:c[16]]]
    return [g.astype(BF16) for g in groups]


GATE_COL0 = 2 * GDN_HEADS


def kernel(x, mem, attn_norm_g, w_in, gdn_conv_w, gdn_a_log, gdn_dt_bias, gdn_norm_g, cmp_pe_k, cmp_w1_k, cmp_w2_k, cmp_pe_v, cmp_w1_v, cmp_w2_v, w_branch_a, w_branch_b, w_mix_out, xattn_norm_g, mem_norm_g, xattn_w_q, xattn_w_kv, xattn_w_o, ffn_norm_g, router_w, router_b, w_gate_up, b_gate_up, w_down, b_down, final_norm_g):
    bsz, seq, d = x.shape
    n = bsz * seq
    depth = w_in.shape[0]
    x2d = x.reshape(n, d)
    for l in range(depth):
        gqkv, gz, small, nq, nkc, nvc, nkv, mab = _inproj(x2d, attn_norm_g[l], _inproj_weights(w_in[l]), [F32] * 8)
        b3 = lambda a: a.reshape(bsz, seq, a.shape[-1])
        o_a = _gdn(b3(gqkv), b3(gz), b3(small), gdn_conv_w[l], gdn_a_log[l], gdn_dt_bias[l], gdn_norm_g[l], bsz, seq)
        q_bf, qr_bf, kv_bf = _nsa_prep(nq, nkv, seq)
        kc = _compress(b3(nkc), cmp_pe_k[l], cmp_w1_k[l], cmp_w2_k[l])
        vc = _compress(b3(nvc), cmp_pe_v[l], cmp_w1_v[l], cmp_w2_v[l])
        o_b = _nsa_attention(b3(q_bf), b3(qr_bf), b3(kv_bf), kc, vc, b3(small), bsz, seq, GATE_COL0)
        memkv = _mem_kv(mem, mem_norm_g[l], xattn_w_kv[l])
        x2, h3, route, cnt = _post_mixer(o_a.reshape(n, -1), o_b.reshape(n, -1), mab, x2d, w_branch_a[l],
                                         w_branch_b[l], w_mix_out[l], xattn_norm_g[l], xattn_w_q[l], memkv,
                                         xattn_w_o[l], ffn_norm_g[l], router_w[l], router_b[l], seq)
        x2d = _moe(h3, x2, route, cnt, w_gate_up[l], b_gate_up[l], w_down[l], b_down[l], final_norm_g,
                   final_norm=(l == depth - 1))
    return x2d.reshape(bsz, seq, d)
```

```python
import functools
import math

import jax
import jax.numpy as jnp
import numpy as np
from jax import lax
from jax.experimental import pallas as pl
from jax.experimental.pallas import tpu as pltpu

F32 = jnp.float32
BF16 = jnp.bfloat16
HIGHEST = lax.Precision.HIGHEST

D_MODEL = 1024
NORM_EPS = 1e-6
L2_EPS = 1e-6
ROPE_THETA = 500000.0
NEG_INF = -1e30

GDN_HEADS = 8
GDN_DK = 128
GDN_DV = 128
GDN_CONV = 4
GDN_CHUNK = 64

NSA_HEADS = 8
NSA_GROUPS = 2
NSA_REP = NSA_HEADS // NSA_GROUPS
NSA_DK = 128
NSA_DV = 128
ROT_DIM = NSA_DK // 4
CMP_LEN = 32
CMP_STRIDE = 16
SEL_LEN = 64
SEL_TOPK = 16
WINDOW = 512
Q_BLOCK = 128
SEL_FORCE = 1e4

MEM_LEN = 256
XATTN_HEADS = 4
XATTN_DH = 128

N_EXPERTS = 32
TOP_K = 4
SWIGLU_LIMIT = 7.0
SWIGLU_ALPHA = 1.702
MOE_BLOCK = 256

LANES = 128
VMEM_LIMIT = 56 * 1024 * 1024
SEL_TILE = 512


def _cparams(n_axes, vmem=VMEM_LIMIT):
    return pltpu.CompilerParams(dimension_semantics=("arbitrary",) * n_axes, vmem_limit_bytes=vmem)


def _dot(a, b, precision=None):
    return jnp.dot(a, b, preferred_element_type=F32, precision=precision)


def _dot_nt(a, b, precision=None):
    return lax.dot_general(a, b, (((1,), (1,)), ((), ())), preferred_element_type=F32, precision=precision)


def _sigmoid(x):
    return 1.0 / (1.0 + jnp.exp(-x))


def _iota(shape, dim):
    return lax.broadcasted_iota(jnp.int32, shape, dim)


def _inproj_kernel(x_ref, g_ref, *refs):
    n = len(refs) // 2
    w_refs, o_refs = refs[:n], refs[n:]
    x = x_ref[...]
    ms = jnp.mean(x * x, axis=-1, keepdims=True)
    h = (x * lax.rsqrt(ms + NORM_EPS) * g_ref[...]).astype(BF16)
    for w_ref, o_ref in zip(w_refs, o_refs):
        o_ref[...] = _dot(h, w_ref[...]).astype(o_ref.dtype)


def _inproj(x2d, gain, weights, out_dtypes, tm=256):
    n, d = x2d.shape
    in_specs = [pl.BlockSpec((tm, d), lambda i: (i, 0)), pl.BlockSpec((1, d), lambda i: (0, 0))]
    in_specs += [pl.BlockSpec(w.shape, lambda i: (0, 0), pipeline_mode=pl.Buffered(1)) for w in weights]
    out_specs = [pl.BlockSpec((tm, w.shape[1]), lambda i: (i, 0)) for w in weights]
    out_shape = [jax.ShapeDtypeStruct((n, w.shape[1]), dt) for w, dt in zip(weights, out_dtypes)]
    return pl.pallas_call(
        _inproj_kernel, grid=(n // tm,), in_specs=in_specs, out_specs=out_specs, out_shape=out_shape,
        compiler_params=_cparams(1), name="inproj")(x2d, gain.reshape(1, d), *weights)


def _rope_tables(seq):
    half = ROT_DIM // 2
    inv_freq = jnp.exp(-math.log(ROPE_THETA) * jnp.arange(half, dtype=F32) * (2.0 / ROT_DIM))
    ang = jnp.arange(seq).astype(F32)[:, None] * inv_freq[None, :]
    cos, sin = jnp.cos(ang), jnp.sin(ang)
    ones = jnp.ones((seq, LANES - ROT_DIM), F32)
    zeros = jnp.zeros((seq, LANES - ROT_DIM), F32)
    zh = jnp.zeros((seq, half), F32)
    c = jnp.concatenate([cos, cos, ones], axis=1)
    sa = jnp.concatenate([zh, sin, zeros], axis=1)
    sb = jnp.concatenate([-sin, zh, zeros], axis=1)
    return c, sa, sb


def _rope(xh, c, sa, sb):
    half = ROT_DIM // 2
    return xh * c + pltpu.roll(xh, half, 1) * sa + pltpu.roll(xh, LANES - half, 1) * sb


MASK_BIG = 2.0 ** 100
Q_PRESCALE = NSA_DK ** -0.5 * math.log2(math.e)


def _nsa_prep_kernel(nq_ref, nkv_ref, c_ref, sa_ref, sb_ref, q_ref, qr_ref, kse_ref, vsa_ref, kw_ref, vwa_ref,
                     *, rows_per_seq):
    tm = nq_ref.shape[0]
    c, sa, sb = c_ref[...], sa_ref[...], sb_ref[...]
    for h in range(NSA_HEADS):
        sl = slice(h * NSA_DK, (h + 1) * NSA_DK)
        xh = nq_ref[:, sl] * Q_PRESCALE
        q_ref[:, sl] = xh.astype(BF16)
        qr_ref[:, sl] = _rope(xh, c, sa, sb).astype(BF16)
    t0 = (pl.program_id(0) % rows_per_seq) * tm
    key_blk = jnp.right_shift(t0 + _iota((tm, 1), 0), int(math.log2(SEL_LEN)))
    blk_neg = jnp.where(key_blk == _iota((1, LANES), 1), -MASK_BIG, 0.0).astype(BF16)
    ones = jnp.ones((tm, LANES), BF16)
    for g in range(NSA_GROUPS):
        col = lambda j: nkv_ref[:, (j * NSA_GROUPS + g) * LANES:(j * NSA_GROUPS + g + 1) * LANES]
        wide = slice(2 * g * LANES, (2 * g + 1) * LANES)
        aux = slice((2 * g + 1) * LANES, (2 * g + 2) * LANES)
        kse_ref[:, wide] = _rope(col(0), c, sa, sb).astype(BF16)
        kse_ref[:, aux] = blk_neg
        vsa_ref[:, wide] = col(1).astype(BF16)
        vsa_ref[:, aux] = ones
        kw_ref[:, g * LANES:(g + 1) * LANES] = _rope(col(2), c, sa, sb).astype(BF16)
        vwa_ref[:, wide] = col(3).astype(BF16)
        vwa_ref[:, aux] = ones


def _nsa_prep(nq, nkv, seq, tm=512):
    n = nq.shape[0]
    assert seq // SEL_LEN <= LANES
    c, sa, sb = _rope_tables(seq)
    nt = seq // tm
    g = NSA_GROUPS
    row = lambda w: pl.BlockSpec((tm, w), lambda i: (i, 0))
    tab = pl.BlockSpec((tm, LANES), lambda i: (i % nt, 0))
    out_w = [nq.shape[1], nq.shape[1], 2 * g * LANES, 2 * g * LANES, g * LANES, 2 * g * LANES]
    return pl.pallas_call(
        functools.partial(_nsa_prep_kernel, rows_per_seq=nt), grid=(n // tm,),
        in_specs=[row(nq.shape[1]), row(nkv.shape[1]), tab, tab, tab],
        out_specs=[row(w) for w in out_w],
        out_shape=[jax.ShapeDtypeStruct((n, w), BF16) for w in out_w],
        compiler_params=_cparams(1), name="nsa_prep")(nq, nkv, c, sa, sb)


def _compress_kernel(x_ref, pea_ref, peb_ref, w1a_ref, w1b_ref, w2_ref, o_ref):
    x = x_ref[0]
    y0 = _dot((x + pea_ref[...]).astype(BF16), w1a_ref[...])
    y1 = _dot((x + peb_ref[...]).astype(BF16), w1b_ref[...])
    n = y1.shape[0]
    pre = y0 + pltpu.roll(y1, n - 1, 0)
    act = pre * _sigmoid(pre)
    o_ref[0] = _dot(act.astype(BF16), w2_ref[...]).astype(o_ref.dtype)


def _compress(x, pe, w1, w2):
    bsz, seq, gd = x.shape
    g = NSA_GROUPS
    d = gd // g
    half = CMP_LEN // 2
    assert CMP_STRIDE == half
    xr = x.reshape(bsz, seq // half, half * gd)
    eye = jnp.eye(g, dtype=F32)

    def expand_w1(w):
        return jnp.einsum("ldk,gh->lgdhk", w.reshape(half, d, d), eye).reshape(half * gd, gd).astype(BF16)

    w1a, w1b = expand_w1(w1[: half * d]), expand_w1(w1[half * d:])
    w2b = jnp.einsum("dk,gh->gdhk", w2, eye).reshape(gd, gd).astype(BF16)
    pea = jnp.broadcast_to(pe[:half, None, :], (half, g, d)).reshape(1, half * gd)
    peb = jnp.broadcast_to(pe[half:, None, :], (half, g, d)).reshape(1, half * gd)
    nrow = seq // half
    full = lambda a: pl.BlockSpec(a.shape, lambda b: (0,) * a.ndim)
    return pl.pallas_call(
        _compress_kernel, grid=(bsz,),
        in_specs=[pl.BlockSpec((1, nrow, half * gd), lambda b: (b, 0, 0)),
                  full(pea), full(peb), full(w1a), full(w1b), full(w2b)],
        out_specs=pl.BlockSpec((1, nrow, gd), lambda b: (b, 0, 0)),
        out_shape=jax.ShapeDtypeStruct((bsz, nrow, gd), BF16),
        compiler_params=_cparams(1), name="nsa_compress")(xr, pea, peb, w1a, w1b, w2b)


def _sel_wmap(seq):
    n_cmp = (seq - CMP_LEN) // CMP_STRIDE + 1
    n_sel = seq // SEL_LEN
    r_s = SEL_LEN // CMP_STRIDE
    r_c = CMP_LEN // CMP_STRIDE
    w = np.zeros((seq // CMP_STRIDE, n_sel), np.float32)
    for s in range(n_sel):
        for m in range(r_s):
            for nn in range(r_c):
                c = s * r_s + m - nn
                if 0 <= c < n_cmp:
                    w[c, s] += 1.0
    return w


def _heads_to_rows(ref, r0):
    return jnp.concatenate([ref[0, pl.ds(r0, Q_BLOCK), r * NSA_DK:(r + 1) * NSA_DK] for r in range(NSA_REP)], axis=0)


def _split3(x):
    hi = x.astype(BF16)
    r1 = x - hi.astype(F32)
    mid = r1.astype(BF16)
    lo = (r1 - mid.astype(F32)).astype(BF16)
    return hi, mid, lo


SELECT_QBLOCKS = 8


def _nsa_select_kernel(q_ref, kc_ref, vc_ref, wmap_ref, oc_ref, nind_ref, imp_ref, *, seq):
    j = pl.program_id(2)
    rows = NSA_REP * Q_BLOCK
    n_sel = seq // SEL_LEN
    n_cmp_pad = seq // CMP_STRIDE
    n_q = SELECT_QBLOCKS * Q_BLOCK
    base = j * n_q
    kc = kc_ref[0]
    vc = vc_ref[0]
    cmp_end = _iota((1, n_cmp_pad), 1) * CMP_STRIDE + (CMP_LEN - 1)

    def per_block(qb_i, carry):
        r0 = pl.multiple_of(qb_i * Q_BLOCK, Q_BLOCK)
        tq1 = base + r0 + _iota((Q_BLOCK, 1), 0)
        s = _dot_nt(_heads_to_rows(q_ref, r0), kc).reshape(NSA_REP, Q_BLOCK, n_cmp_pad)
        mask = (cmp_end <= tq1)[None]
        s = jnp.where(mask, s, NEG_INF)
        p = jnp.where(mask, jnp.exp2(s - jnp.max(s, axis=-1, keepdims=True)), 0.0)
        denom = jnp.maximum(jnp.sum(p, axis=-1, keepdims=True), 1e-30)
        p_c = p / denom
        o_c = _dot(p_c.reshape(rows, n_cmp_pad).astype(BF16), vc)
        for r in range(NSA_REP):
            oc_ref[0, pl.ds(r0, Q_BLOCK), r * NSA_DV:(r + 1) * NSA_DV] = o_c[r * Q_BLOCK:(r + 1) * Q_BLOCK]
        p_sum = p_c[0]
        for r in range(1, NSA_REP):
            p_sum = p_sum + p_c[r]
        hi, mid, lo = _split3(p_sum)
        w = wmap_ref[...]
        imp_ref[pl.ds(r0, Q_BLOCK), :] = _dot(hi, w) + _dot(mid, w) + _dot(lo, w)
        return carry

    lax.fori_loop(0, SELECT_QBLOCKS, per_block, 0)

    n_pick = SEL_TOPK - 3
    tq = base + _iota((n_q, 1), 0)
    sel_ids = _iota((1, LANES), 1)
    cur = jnp.right_shift(tq, int(math.log2(SEL_LEN)))
    forced = (sel_ids == 0) | (sel_ids == cur) | (sel_ids == cur - 1)
    causal_blk = sel_ids * SEL_LEN <= tq
    cand = jnp.where(causal_blk & jnp.logical_not(forced), imp_ref[...], -jnp.inf)

    def write(picked):
        keep = forced | (picked > 0.5)
        nind_ref[0, 0] = jnp.where(causal_blk & keep, 0.0, 1.0).astype(BF16)

    score = cand
    picked = jnp.zeros((n_q, LANES), F32)
    for _ in range(n_pick):
        mx = jnp.max(score, axis=-1, keepdims=True)
        pick = (score == mx) & (mx > -jnp.inf)
        picked = jnp.where(pick, 1.0, picked)
        score = jnp.where(pick, -jnp.inf, score)
    write(picked)
    most = jnp.max(jnp.sum(picked, axis=-1, keepdims=True))

    @pl.when(most > n_pick)
    def _():
        lane = _iota((n_q, LANES), 1).astype(F32)
        score = cand
        picked = jnp.zeros((n_q, LANES), F32)
        for _ in range(n_pick):
            mx = jnp.max(score, axis=-1, keepdims=True)
            hit = (score == mx) & (mx > -jnp.inf)
            first = jnp.min(jnp.where(hit, lane, float(LANES)), axis=-1, keepdims=True)
            pick = lane == first
            picked = jnp.where(pick, 1.0, picked)
            score = jnp.where(pick, -jnp.inf, score)
        write(picked)


def _nsa_select(q, kc, vc, bsz, seq):
    g = NSA_GROUPS
    n_q = SELECT_QBLOCKS * Q_BLOCK
    wmap = jnp.asarray(_sel_wmap(seq), BF16)
    wmap = jnp.pad(wmap, ((0, 0), (0, LANES - wmap.shape[1])))
    cspec = pl.BlockSpec((1, seq // CMP_STRIDE, LANES), lambda b, gg, j: (b, 0, gg))
    kernel = functools.partial(_nsa_select_kernel, seq=seq)
    return pl.pallas_call(
        kernel, grid=(bsz, g, seq // n_q),
        in_specs=[pl.BlockSpec((1, n_q, NSA_REP * NSA_DK), lambda b, gg, j: (b, j, gg)), cspec, cspec,
                  pl.BlockSpec(wmap.shape, lambda b, gg, j: (0, 0))],
        out_specs=[pl.BlockSpec((1, n_q, NSA_REP * NSA_DV), lambda b, gg, j: (b, j, gg)),
                   pl.BlockSpec((1, 1, n_q, LANES), lambda b, gg, j: (b, gg, j, 0))],
        out_shape=[jax.ShapeDtypeStruct((bsz, seq, NSA_HEADS * NSA_DV), F32),
                   jax.ShapeDtypeStruct((bsz, g, seq, LANES), BF16)],
        scratch_shapes=[pltpu.VMEM((n_q, LANES), F32)],
        compiler_params=_cparams(3), name="nsa_select")(q, kc, vc, wmap)


def _nsa_attn_kernel(qr_ref, nind_ref, oc_ref, kse_ref, vsa_ref, kw_ref, vwa_ref, gate_ref, o_ref, m_ref, acc_ref,
                     *, gate_col0):
    g = pl.program_id(1)
    i = pl.program_id(2)
    q0 = i * Q_BLOCK
    rows = NSA_REP * Q_BLOCK
    tq1 = q0 + _iota((Q_BLOCK, 1), 0)

    qrb = _heads_to_rows(qr_ref, 0)
    lhs = jnp.concatenate([qrb, jnp.concatenate([nind_ref[0, 0]] * NSA_REP, axis=0)], axis=1)

    m_ref[...] = jnp.full(m_ref.shape, -MASK_BIG, F32)
    acc_ref[...] = jnp.zeros(acc_ref.shape, F32)

    def tile_terms(kt, causal):
        k0 = pl.multiple_of(kt * SEL_TILE, SEL_TILE)
        s = _dot_nt(lhs, kse_ref[0, pl.ds(k0, SEL_TILE), :])
        if causal:
            bias = jnp.where(k0 + _iota((1, SEL_TILE), 1) <= tq1, 0.0, -MASK_BIG)
            s = (s.reshape(NSA_REP, Q_BLOCK, SEL_TILE) + bias[None]).reshape(rows, SEL_TILE)
        m_t = jnp.max(s, axis=-1, keepdims=True)
        p = jnp.exp2(s - m_t).astype(BF16)
        return m_t, _dot(p, vsa_ref[0, pl.ds(k0, SEL_TILE), :])

    def merge(terms):
        m_new = m_ref[...]
        for m_t, _ in terms:
            m_new = jnp.maximum(m_new, m_t)
        acc = jnp.exp2(m_ref[...] - m_new) * acc_ref[...]
        for m_t, pv in terms:
            acc = acc + jnp.exp2(m_t - m_new) * pv
        acc_ref[...] = acc
        m_ref[...] = m_new

    def tile_pair(kp, carry):
        merge([tile_terms(2 * kp, False), tile_terms(2 * kp + 1, False)])
        return carry

    n_full = q0 // SEL_TILE
    lax.fori_loop(0, n_full // 2, tile_pair, 0)

    @pl.when(n_full % 2 == 1)
    def _():
        merge([tile_terms(n_full - 1, False)])

    merge([tile_terms(n_full, True)])
    o_s = acc_ref[:, :NSA_DV] / acc_ref[:, NSA_DV:]

    wlen = WINDOW + Q_BLOCK
    w0 = pl.multiple_of(jnp.maximum(q0 - WINDOW, 0), Q_BLOCK)
    s_w = _dot_nt(qrb, kw_ref[0, pl.ds(w0, wlen), :]).reshape(NSA_REP, Q_BLOCK, wlen)
    dlt = tq1 - (w0 + _iota((1, wlen), 1))
    band = jnp.where((dlt >= 0) & (dlt < WINDOW), 0.0, -MASK_BIG)
    s_w = (s_w + band[None]).reshape(rows, wlen)
    p_w = jnp.exp2(s_w - jnp.max(s_w, axis=-1, keepdims=True)).astype(BF16)
    pv_w = _dot(p_w, vwa_ref[0, pl.ds(w0, wlen), :])
    o_w = pv_w[:, :NSA_DV] / pv_w[:, NSA_DV:]

    gts = _sigmoid(gate_ref[0])
    for r in range(NSA_REP):
        rs = slice(r * Q_BLOCK, (r + 1) * Q_BLOCK)
        cols = []
        for j in range(3):
            col = gate_col0 + (g * NSA_REP + r) * 3 + j
            onehot = _iota((1, LANES), 1) == col
            cols.append(jnp.sum(jnp.where(onehot, gts, 0.0), axis=-1, keepdims=True))
        o_c = oc_ref[0, :, r * NSA_DV:(r + 1) * NSA_DV]
        out = cols[0] * o_c + cols[1] * o_s[rs] + cols[2] * o_w[rs]
        o_ref[0, :, r * NSA_DV:(r + 1) * NSA_DV] = out.astype(o_ref.dtype)


def _nsa_attention(qr, nind, oc, kse, vsa, kw, vwa, small, bsz, seq, gate_col0):
    g = NSA_GROUPS
    rows = NSA_REP * Q_BLOCK
    qspec = pl.BlockSpec((1, Q_BLOCK, NSA_REP * NSA_DK), lambda b, gg, i: (b, i, gg))
    wide = pl.BlockSpec((1, seq, 2 * LANES), lambda b, gg, i: (b, 0, gg))
    kernel = functools.partial(_nsa_attn_kernel, gate_col0=gate_col0)
    return pl.pallas_call(
        kernel, grid=(bsz, g, seq // Q_BLOCK),
        in_specs=[qspec,
                  pl.BlockSpec((1, 1, Q_BLOCK, LANES), lambda b, gg, i: (b, gg, i, 0)),
                  qspec, wide, wide,
                  pl.BlockSpec((1, seq, LANES), lambda b, gg, i: (b, 0, gg)),
                  wide,
                  pl.BlockSpec((1, Q_BLOCK, LANES), lambda b, gg, i: (b, i, 0))],
        out_specs=qspec,
        out_shape=jax.ShapeDtypeStruct((bsz, seq, NSA_HEADS * NSA_DV), BF16),
        scratch_shapes=[pltpu.VMEM((rows, 1), F32), pltpu.VMEM((rows, 2 * NSA_DV), F32)],
        compiler_params=_cparams(3), name="nsa_attn")(qr, nind, oc, kse, vsa, kw, vwa, small)


GDN_UNIT = 2 * GDN_CHUNK
CONV_HALO = 8


def _lcat(a, b):
    return jnp.concatenate([a, b], axis=1)


def _bdiag(a, b):
    za = jnp.zeros((a.shape[0], b.shape[1]), a.dtype)
    zb = jnp.zeros((b.shape[0], a.shape[1]), b.dtype)
    return jnp.concatenate([_lcat(a, za), _lcat(zb, b)], axis=0)


def _hilo(x):
    hi = x.astype(BF16)
    return hi, (x - hi.astype(F32)).astype(BF16)


def _pair_rhs(x):
    m = x.shape[1] // 2
    hi, lo = _hilo(x)
    return _bdiag(hi[:, :m], hi[:, m:]), _bdiag(lo[:, :m], lo[:, m:])


def _mm3(lhs, rhs):
    return _dot(lhs[0], rhs[0]) + _dot(lhs[0], rhs[1]) + _dot(lhs[1], rhs[0])


def _pair_lower_inverse(mats, eye, b16, b32):
    sq = lambda xs: [_mm3(_hilo(x), _pair_rhs(x)) for x in xs]
    mul = lambda xs, ys: [_mm3(_hilo(x), _pair_rhs(y)) for x, y in zip(xs, ys)]
    d = [jnp.where(b16, a, 0.0) for a in mats]
    d2 = sq(d)
    d4 = sq(d2)
    d8 = sq(d4)
    t = mul([eye - x for x in d], [eye + x for x in d2])
    t = mul(t, [eye + x for x in d4])
    t = mul(t, [eye + x for x in d8])
    for pick in (lambda a: jnp.where(b32 & jnp.logical_not(b16), a, 0.0), lambda a: jnp.where(b32, 0.0, a)):
        inner = mul([pick(a) for a in mats], t)
        t = [x - y for x, y in zip(t, mul(t, inner))]
    return t


def _gdn_kernel(x_ref, halo_ref, z_ref, small_ref, convw_ref, alog_ref, dtb_ref, ng_ref, o_ref, s_ref):
    i = pl.program_id(1)
    u_len = GDN_UNIT
    c_len = GDN_CHUNK
    dk, dv = GDN_DK, GDN_DV
    hq = GDN_HEADS * dk

    @pl.when(i == 0)
    def _():
        s_ref[...] = jnp.zeros(s_ref.shape, F32)

    row = _iota((u_len, 2 * u_len), 0)
    col = _iota((u_len, 2 * u_len), 1) & (u_len - 1)
    same = jnp.right_shift(row, 6) == jnp.right_shift(col, 6)
    incl = same & (row >= col)
    strict = same & (row > col)
    b16 = jnp.right_shift(row, 4) == jnp.right_shift(col, 4)
    b32 = jnp.right_shift(row, 5) == jnp.right_shift(col, 5)
    eye = (row == col).astype(F32)
    eye1 = eye[:, :u_len] > 0.5
    rcol = _iota((u_len, 1), 0)
    lane = _iota((1, LANES), 1)
    first_rows = rcol < c_len
    first_cols = (_iota((1, 2 * u_len), 1) & (u_len - 1)) < c_len
    head0 = _iota((1, 2 * dv), 1) < dv
    halo_on = (i > 0).astype(F32)

    sm = small_ref[0]
    beta_all = _sigmoid(sm)
    xs = sm + dtb_ref[...]
    softplus = jnp.maximum(xs, 0.0) + jnp.log1p(jnp.exp(-jnp.abs(xs)))
    ld_all = -jnp.exp(alog_ref[...]) * softplus
    cum = incl[:, :u_len].astype(BF16)
    ld_hi, ld_mid, ld_lo = _split3(ld_all)
    gc_all = _dot(cum, ld_hi) + _dot(cum, ld_mid) + _dot(cum, ld_lo)

    def pick_col(a, c):
        return jnp.sum(jnp.where(lane == c, a, 0.0), axis=-1, keepdims=True)

    def conv_silu(c0):
        xf = jnp.concatenate([halo_ref[0, :, c0:c0 + LANES] * halo_on, x_ref[0, :, c0:c0 + LANES]], axis=0)
        w = convw_ref[:, c0:c0 + LANES]
        y = w[0:1] * xf[CONV_HALO - 3:CONV_HALO - 3 + u_len]
        for j in range(1, GDN_CONV):
            off = CONV_HALO - (GDN_CONV - 1) + j
            y = y + w[j:j + 1] * xf[off:off + u_len]
        return y * _sigmoid(y)

    def head_terms(h):
        q = conv_silu(h * dk)
        k = conv_silu(hq + h * dk)
        v = conv_silu(2 * hq + h * dv)
        q = q * lax.rsqrt(jnp.sum(q * q, axis=-1, keepdims=True) + L2_EPS) * (dk ** -0.5)
        k = k * lax.rsqrt(jnp.sum(k * k, axis=-1, keepdims=True) + L2_EPS)
        beta = pick_col(beta_all, h)
        gcol = pick_col(gc_all, GDN_HEADS + h)
        grow = jnp.sum(jnp.where(eye1, gcol, 0.0), axis=0, keepdims=True)
        gl0 = jnp.sum(jnp.where(rcol == c_len - 1, gcol, 0.0), axis=0, keepdims=True)
        gl1 = jnp.sum(jnp.where(rcol == u_len - 1, gcol, 0.0), axis=0, keepdims=True)
        e_g = jnp.exp(gcol)
        kb = k * beta
        return dict(q=q, kb=kb, k_t=k.T, vb_kw=_lcat(v * beta, kb * e_g), qd=q * e_g,
                    diff=gcol - grow, kd_scale=jnp.exp(jnp.where(first_cols[:, :u_len], gl0, gl1) - grow),
                    g0=jnp.exp(gl0), g1=jnp.exp(gl1))

    n_pairs = GDN_HEADS // 2
    prs = range(n_pairs)
    heads = [head_terms(h) for h in range(GDN_HEADS)]
    pa = [heads[2 * p] for p in prs]
    pb = [heads[2 * p + 1] for p in prs]
    decay = [jnp.where(incl, jnp.exp(jnp.where(incl, _lcat(a["diff"], b["diff"]), 0.0)), 0.0)
             for a, b in zip(pa, pb)]
    kt_rhs = [_pair_rhs(_lcat(a["k_t"], b["k_t"])) for a, b in zip(pa, pb)]
    gram = [_mm3(_hilo(_lcat(a["kb"], b["kb"])), r) for a, b, r in zip(pa, pb, kt_rhs)]
    t_mat = _pair_lower_inverse([jnp.where(strict, g_ * d_, 0.0) for g_, d_ in zip(gram, decay)], eye, b16, b32)
    vk = [(_hilo(a["vb_kw"]), _hilo(b["vb_kw"])) for a, b in zip(pa, pb)]
    uw = [_mm3(_hilo(t), (_bdiag(va[0], vb[0]), _bdiag(va[1], vb[1])))
          for t, (va, vb) in zip(t_mat, vk)]
    u = [_lcat(x[:, :dv], x[:, dv + dk:2 * dv + dk]) for x in uw]
    w = [_lcat(x[:, dv:dv + dk], x[:, 2 * dv + dk:]).astype(BF16) for x in uw]
    qk = [jnp.where(incl, _dot(_lcat(a["q"], b["q"]).astype(BF16), r[0]) * d_, 0.0).astype(BF16)
          for a, b, r, d_ in zip(pa, pb, kt_rhs, decay)]
    w_qd = [jnp.concatenate([w_, _lcat(a["qd"], b["qd"]).astype(BF16)], axis=0)
            for w_, a, b in zip(w, pa, pb)]
    kd = [_lcat(a["k_t"] * a["kd_scale"], b["k_t"] * b["kd_scale"]) for a, b in zip(pa, pb)]
    qk_kd = [[jnp.concatenate([qk_, jnp.where(first_cols, x, 0.0).astype(BF16)], axis=0) for qk_, x in zip(qk, kd)],
             [jnp.concatenate([qk_, jnp.where(first_cols, 0.0, x).astype(BF16)], axis=0) for qk_, x in zip(qk, kd)]]

    s = [s_ref[p] for p in prs]
    outs = []
    for c in range(2):
        gkey = "g%d" % c
        sb = [x.astype(BF16) for x in s]
        r1 = [_dot(w_qd[p], _bdiag(sb[p][:, :dv], sb[p][:, dv:])) for p in prs]
        vn = [(u[p] - r1[p][:u_len]).astype(BF16) for p in prs]
        r2 = [_dot(qk_kd[c][p], _bdiag(vn[p][:, :dv], vn[p][:, dv:])) for p in prs]
        outs.append([r1[p][u_len:] + r2[p][:u_len] for p in prs])
        s = [s[p] * jnp.where(head0, pa[p][gkey], pb[p][gkey]) + r2[p][u_len:] for p in prs]
    for p in prs:
        s_ref[p] = s[p]
        o_pair = jnp.where(first_rows, outs[0][p], outs[1][p])
        for j in range(2):
            h = 2 * p + j
            o = o_pair[:, j * dv:(j + 1) * dv]
            o = o * lax.rsqrt(jnp.mean(o * o, axis=-1, keepdims=True) + NORM_EPS) * ng_ref[...]
            z = z_ref[0, :, h * dv:(h + 1) * dv]
            o_ref[0, :, h * dv:(h + 1) * dv] = (o * (z * _sigmoid(z))).astype(o_ref.dtype)


def _gdn(gqkv, gz, small, conv_w, a_log, dt_bias, norm_g, bsz, seq):
    u_len = GDN_UNIT
    c_all = gqkv.shape[-1]
    hv = GDN_HEADS * GDN_DV
    pad = jnp.zeros((GDN_HEADS,), F32)
    tail = jnp.zeros((LANES - 2 * GDN_HEADS,), F32)
    alog_row = jnp.concatenate([pad, a_log.astype(F32), tail]).reshape(1, LANES)
    dtb_row = jnp.concatenate([pad, dt_bias.astype(F32), tail]).reshape(1, LANES)
    per_halo = u_len // CONV_HALO
    return pl.pallas_call(
        _gdn_kernel, grid=(bsz, seq // u_len),
        in_specs=[pl.BlockSpec((1, u_len, c_all), lambda b, i: (b, i, 0)),
                  pl.BlockSpec((1, CONV_HALO, c_all), lambda b, i: (b, jnp.maximum(i * per_halo - 1, 0), 0)),
                  pl.BlockSpec((1, u_len, hv), lambda b, i: (b, i, 0)),
                  pl.BlockSpec((1, u_len, LANES), lambda b, i: (b, i, 0)),
                  pl.BlockSpec(conv_w.shape, lambda b, i: (0, 0)),
                  pl.BlockSpec((1, LANES), lambda b, i: (0, 0)),
                  pl.BlockSpec((1, LANES), lambda b, i: (0, 0)),
                  pl.BlockSpec((1, GDN_DV), lambda b, i: (0, 0))],
        out_specs=pl.BlockSpec((1, u_len, hv), lambda b, i: (b, i, 0)),
        out_shape=jax.ShapeDtypeStruct((bsz, seq, hv), BF16),
        scratch_shapes=[pltpu.VMEM((GDN_HEADS // 2, GDN_DK, 2 * GDN_DV), F32)],
        compiler_params=_cparams(2), name="gdn")(gqkv, gqkv, gz, small, conv_w, alog_row, dtb_row,
                                                  norm_g.reshape(1, GDN_DV))


def _rms(x, gain):
    return x * lax.rsqrt(jnp.mean(x * x, axis=-1, keepdims=True) + NORM_EPS) * gain


def _mem_kv_kernel(m_ref, g_ref, w_ref, o_ref):
    h = _rms(m_ref[0], g_ref[...]).astype(BF16)
    o_ref[0] = _dot(h, w_ref[...]).astype(o_ref.dtype)


def _mem_kv(mem, gain, w_kv):
    bsz, m_len, d = mem.shape
    w = w_kv.astype(BF16)
    return pl.pallas_call(
        _mem_kv_kernel, grid=(bsz,),
        in_specs=[pl.BlockSpec((1, m_len, d), lambda b: (b, 0, 0)), pl.BlockSpec((1, d), lambda b: (0, 0)),
                  pl.BlockSpec(w.shape, lambda b: (0, 0))],
        out_specs=pl.BlockSpec((1, m_len, w.shape[1]), lambda b: (b, 0, 0)),
        out_shape=jax.ShapeDtypeStruct((bsz, m_len, w.shape[1]), BF16),
        compiler_params=_cparams(1), name="mem_kv")(mem, gain.reshape(1, d), w)


ROUTE_E, ROUTE_G, ROUTE_R = 0, TOP_K, 2 * TOP_K


def _post_mixer_kernel(oa_ref, ob_ref, mab_ref, x_ref, wa_ref, wb_ref, wmix_ref, gx_ref, wq_ref, km_ref, vm_ref,
                       wo_ref, gf_ref, rw_ref, rb_ref, x2_ref, h3_ref, route_ref, cnt_ref, run_ref):
    i = pl.program_id(0)
    tm, d = x_ref.shape

    @pl.when(i == 0)
    def _():
        run_ref[...] = jnp.zeros(run_ref.shape, F32)

    ya = _dot(oa_ref[...], wa_ref[...])
    yb = _dot(ob_ref[...], wb_ref[...])
    mixed = _sigmoid(mab_ref[:, :d]) * ya + _sigmoid(mab_ref[:, d:]) * yb
    x1 = x_ref[...] + _dot(mixed.astype(BF16), wmix_ref[...])

    hq = _rms(x1, gx_ref[...]).astype(BF16)
    q = _dot(hq, wq_ref[...])
    outs = []
    for hd in range(XATTN_HEADS):
        sl = slice(hd * XATTN_DH, (hd + 1) * XATTN_DH)
        s = _dot_nt(q[:, sl].astype(BF16), km_ref[0, :, sl]) * (XATTN_DH ** -0.5)
        p = jnp.exp(s - jnp.max(s, axis=-1, keepdims=True))
        p = p / jnp.sum(p, axis=-1, keepdims=True)
        outs.append(_dot(p.astype(BF16), vm_ref[0, :, sl]))
    o = jnp.concatenate(outs, axis=-1).astype(BF16)
    x2 = x1 + _dot(o, wo_ref[...])
    x2_ref[...] = x2

    h3 = _rms(x2, gf_ref[...])
    h3_ref[...] = h3
    logits = _dot(h3, rw_ref[...], precision=HIGHEST) + rb_ref[...]
    lane = _iota((tm, LANES), 1).astype(F32)
    work = logits
    vals, idxs = [], []
    for _ in range(TOP_K):
        mx = jnp.max(work, axis=-1, keepdims=True)
        first = jnp.min(jnp.where(work == mx, lane, float(LANES)), axis=-1, keepdims=True)
        vals.append(mx)
        idxs.append(first)
        work = jnp.where(lane == first, -jnp.inf, work)
    exps = [jnp.exp(v - vals[0]) for v in vals]
    den = exps[0]
    for e in exps[1:]:
        den = den + e
    onehot = jnp.zeros((tm, LANES), F32)
    for idx in idxs:
        onehot = onehot + (lane == idx).astype(F32)
    earlier = (_iota((tm, tm), 0) > _iota((tm, tm), 1)).astype(BF16)
    before = run_ref[...] + _dot(earlier, onehot.astype(BF16))
    route = jnp.zeros((tm, LANES), F32)
    for k in range(TOP_K):
        rank = jnp.sum(jnp.where(lane == idxs[k], before, 0.0), axis=-1, keepdims=True)
        route = jnp.where(lane == float(ROUTE_E + k), idxs[k], route)
        route = jnp.where(lane == float(ROUTE_G + k), exps[k] / den, route)
        route = jnp.where(lane == float(ROUTE_R + k), rank, route)
    route_ref[...] = route
    run_ref[...] = run_ref[...] + jnp.sum(onehot, axis=0, keepdims=True)
    cnt_ref[...] = run_ref[...]


def _post_mixer(o_a, o_b, mab, x2d, w_a, w_b, w_mix, gx, w_q, memkv, w_o, gf, router_w, router_b, seq, tm=256):
    n, d = x2d.shape
    xd = w_q.shape[1]
    m_len = memkv.shape[1]
    per_b = seq // tm
    n_exp = router_w.shape[1]
    rw = jnp.concatenate([router_w.astype(F32), jnp.zeros((d, LANES - n_exp), F32)], axis=1)
    rb = jnp.concatenate([router_b.astype(F32), jnp.full((LANES - n_exp,), NEG_INF, F32)]).reshape(1, LANES)
    row = lambda w: pl.BlockSpec((tm, w), lambda i: (i, 0))
    full = lambda a: pl.BlockSpec(a.shape, lambda i: (0,) * a.ndim)
    wa, wb, wm, wq, wo = (w.astype(BF16) for w in (w_a, w_b, w_mix, w_q, w_o))
    gx2, gf2 = gx.reshape(1, d), gf.reshape(1, d)
    return pl.pallas_call(
        _post_mixer_kernel, grid=(n // tm,),
        in_specs=[row(d), row(d), row(2 * d), row(d), full(wa), full(wb), full(wm), full(gx2), full(wq),
                  pl.BlockSpec((1, m_len, xd), lambda i: (i // per_b, 0, 0)),
                  pl.BlockSpec((1, m_len, xd), lambda i: (i // per_b, 0, 1)),
                  full(wo), full(gf2), full(rw), full(rb)],
        out_specs=[row(d), row(d), row(LANES), pl.BlockSpec((1, LANES), lambda i: (0, 0))],
        out_shape=[jax.ShapeDtypeStruct((n, d), F32), jax.ShapeDtypeStruct((n, d), F32),
                   jax.ShapeDtypeStruct((n, LANES), F32), jax.ShapeDtypeStruct((1, LANES), F32)],
        scratch_shapes=[pltpu.VMEM((1, LANES), F32)],
        compiler_params=_cparams(1), name="post_mixer")(o_a, o_b, mab, x2d, wa, wb, wm, gx2, wq, memkv, memkv, wo,
                                                        gf2, rw, rb)


MOE_TOK_TILE = 128


def _dispatch_kernel(dest_ref, h_ref, xs_in_ref, xs_ref, sem):
    del xs_in_ref
    tt = h_ref.shape[0]

    def row_copy(t, dst_row):
        return pltpu.make_async_copy(h_ref.at[pl.ds(t, 1)], xs_ref.at[pl.ds(dst_row, 1)], sem)

    def start(t, c):
        for k in range(TOP_K):
            row_copy(t, dest_ref[0, 0, t * TOP_K + k]).start()
        return c

    def wait(t, c):
        for k in range(TOP_K):
            row_copy(t, dest_ref[0, 0, t * TOP_K + k]).wait()
        return c

    lax.fori_loop(0, tt, start, 0)
    lax.fori_loop(0, tt, wait, 0)


def _dispatch(h3, dest, n_rows):
    n, d = h3.shape
    tt = MOE_TOK_TILE
    dest3 = dest.reshape(n // tt, 1, tt * TOP_K)
    xs0 = jnp.zeros((n_rows, d), h3.dtype)
    return pl.pallas_call(
        _dispatch_kernel, grid=(n // tt,),
        in_specs=[pl.BlockSpec((1, 1, tt * TOP_K), lambda i: (i, 0, 0), memory_space=pltpu.SMEM),
                  pl.BlockSpec((tt, d), lambda i: (i, 0)),
                  pl.BlockSpec(memory_space=pl.ANY)],
        out_specs=pl.BlockSpec(memory_space=pl.ANY),
        out_shape=jax.ShapeDtypeStruct((n_rows, d), h3.dtype),
        scratch_shapes=[pltpu.SemaphoreType.DMA(())],
        input_output_aliases={2: 0},
        compiler_params=_cparams(1), name="moe_dispatch")(dest3, h3, xs0)


def _expert_kernel(be_ref, nu_ref, xs_ref, wgu_ref, bgu_ref, wdn_ref, bdn_ref, y_ref, wgu_bf, wdn_bf):
    j = pl.program_id(0)
    d_exp = wdn_ref.shape[1]
    used = j < nu_ref[0]
    fresh = (j == 0) | (be_ref[j] != be_ref[jnp.maximum(j - 1, 0)])
    chunk = 128

    @pl.when(used & fresh)
    def _():
        def cast_gu(c, carry):
            r0 = pl.multiple_of(c * chunk, chunk)
            wgu_bf[pl.ds(r0, chunk), :] = wgu_ref[0, pl.ds(r0, chunk), :].astype(BF16)
            return carry

        def cast_dn(c, carry):
            r0 = pl.multiple_of(c * chunk, chunk)
            wdn_bf[pl.ds(r0, chunk), :] = wdn_ref[0, pl.ds(r0, chunk), :].astype(BF16)
            return carry

        lax.fori_loop(0, wgu_bf.shape[0] // chunk, cast_gu, 0)
        lax.fori_loop(0, wdn_bf.shape[0] // chunk, cast_dn, 0)

    @pl.when(used)
    def _():
        gu = _dot(xs_ref[...].astype(BF16), wgu_bf[...]) + bgu_ref[0]
        gate = jnp.minimum(gu[:, :d_exp], SWIGLU_LIMIT)
        up = jnp.clip(gu[:, d_exp:], -SWIGLU_LIMIT, SWIGLU_LIMIT)
        act = gate * _sigmoid(gate * SWIGLU_ALPHA) * (up + 1.0)
        y_ref[...] = _dot(act.astype(BF16), wdn_bf[...]) + bdn_ref[0]

    @pl.when(jnp.logical_not(used))
    def _():
        y_ref[...] = jnp.zeros(y_ref.shape, y_ref.dtype)


def _experts(xs, block_exp, n_used, w_gu, b_gu, w_dn, b_dn):
    n_rows, d = xs.shape
    n_exp, _, gu_w = w_gu.shape
    d_exp = w_dn.shape[1]
    n_blocks = n_rows // MOE_BLOCK
    grid_spec = pltpu.PrefetchScalarGridSpec(
        num_scalar_prefetch=2, grid=(n_blocks,),
        in_specs=[pl.BlockSpec((MOE_BLOCK, d), lambda j, be, nu: (j, 0)),
                  pl.BlockSpec((1, d, gu_w), lambda j, be, nu: (be[j], 0, 0)),
                  pl.BlockSpec((1, 1, gu_w), lambda j, be, nu: (be[j], 0, 0)),
                  pl.BlockSpec((1, d_exp, d), lambda j, be, nu: (be[j], 0, 0)),
                  pl.BlockSpec((1, 1, d), lambda j, be, nu: (be[j], 0, 0))],
        out_specs=pl.BlockSpec((MOE_BLOCK, d), lambda j, be, nu: (j, 0)),
        scratch_shapes=[pltpu.VMEM((d, gu_w), BF16), pltpu.VMEM((d_exp, d), BF16)])
    return pl.pallas_call(
        _expert_kernel, grid_spec=grid_spec,
        out_shape=jax.ShapeDtypeStruct((n_rows, d), F32),
        compiler_params=_cparams(1), name="moe_experts")(
            block_exp, n_used, xs, w_gu, b_gu.reshape(n_exp, 1, gu_w), w_dn, b_dn.reshape(n_exp, 1, d))


def _combine_kernel(dest_ref, y_ref, x_ref, route_ref, g_ref, o_ref, buf, sem, *, final_norm):
    tt = x_ref.shape[0]

    def row_copy(t, k, src_row):
        return pltpu.make_async_copy(y_ref.at[pl.ds(src_row, 1)], buf.at[k, pl.ds(t, 1)], sem)

    def start(t, c):
        for k in range(TOP_K):
            row_copy(t, k, dest_ref[0, 0, t * TOP_K + k]).start()
        return c

    def wait(t, c):
        for k in range(TOP_K):
            row_copy(t, k, dest_ref[0, 0, t * TOP_K + k]).wait()
        return c

    lax.fori_loop(0, tt, start, 0)
    lax.fori_loop(0, tt, wait, 0)
    lane = _iota((1, LANES), 1)
    rt = route_ref[...]
    moe = jnp.zeros(x_ref.shape, F32)
    for k in range(TOP_K):
        gate = jnp.sum(jnp.where(lane == ROUTE_G + k, rt, 0.0), axis=-1, keepdims=True)
        moe = moe + gate * buf[k]
    out = x_ref[...] + moe
    o_ref[...] = _rms(out, g_ref[...]) if final_norm else out


def _combine(y, dest, x2, route, final_g, final_norm):
    n, d = x2.shape
    tt = MOE_TOK_TILE
    dest3 = dest.reshape(n // tt, 1, tt * TOP_K)
    kernel = functools.partial(_combine_kernel, final_norm=final_norm)
    return pl.pallas_call(
        kernel, grid=(n // tt,),
        in_specs=[pl.BlockSpec((1, 1, tt * TOP_K), lambda i: (i, 0, 0), memory_space=pltpu.SMEM),
                  pl.BlockSpec(memory_space=pl.ANY),
                  pl.BlockSpec((tt, d), lambda i: (i, 0)),
                  pl.BlockSpec((tt, LANES), lambda i: (i, 0)),
                  pl.BlockSpec((1, d), lambda i: (0, 0))],
        out_specs=pl.BlockSpec((tt, d), lambda i: (i, 0)),
        out_shape=jax.ShapeDtypeStruct((n, d), F32),
        scratch_shapes=[pltpu.VMEM((TOP_K, tt, d), F32), pltpu.SemaphoreType.DMA(())],
        compiler_params=_cparams(1), name="moe_combine")(dest3, y, x2, route, final_g.reshape(1, d))


def _moe(h3, x2, route, cnt, w_gu, b_gu, w_dn, b_dn, final_g, final_norm):
    n, d = h3.shape
    n_exp = w_gu.shape[0]
    e = route[:, ROUTE_E:ROUTE_E + TOP_K].astype(jnp.int32)
    rank = route[:, ROUTE_R:ROUTE_R + TOP_K].astype(jnp.int32)
    counts = cnt[0, :n_exp].astype(jnp.int32)
    padded = ((counts + MOE_BLOCK - 1) // MOE_BLOCK) * MOE_BLOCK
    pend = jnp.cumsum(padded)
    dest = (pend - padded)[e] + rank
    n_rows = n * TOP_K + n_exp * MOE_BLOCK
    n_blocks = n_rows // MOE_BLOCK
    starts = jnp.arange(n_blocks, dtype=jnp.int32) * MOE_BLOCK
    block_exp = jnp.minimum(jnp.sum((pend[None, :] <= starts[:, None]).astype(jnp.int32), axis=1), n_exp - 1)
    n_used = (pend[-1:] // MOE_BLOCK).astype(jnp.int32)
    xs = _dispatch(h3, dest, n_rows)
    y = _experts(xs, block_exp, n_used, w_gu, b_gu, w_dn, b_dn)
    return _combine(y, dest, x2, route, final_g, final_norm)


def _inproj_weights(w):
    gh = GDN_HEADS
    c = np.cumsum([0, gh * GDN_DK, gh * GDN_DK, gh * GDN_DV, gh * GDN_DV, gh, gh, NSA_HEADS * NSA_DK,
                   NSA_GROUPS * NSA_DK, NSA_GROUPS * NSA_DV, NSA_GROUPS * NSA_DK, NSA_GROUPS * NSA_DV,
                   NSA_GROUPS * NSA_DK, NSA_GROUPS * NSA_DV, NSA_HEADS * 3, D_MODEL, D_MODEL])
    n_small = 2 * gh + NSA_HEADS * 3
    small = jnp.concatenate([w[:, c[4]:c[6]], w[:, c[13]:c[14]], jnp.zeros((w.shape[0], LANES - n_small), w.dtype)],
                            axis=1)
    groups = [w[:, c[0]:c[3]],
              w[:, c[3]:c[4]],
              small,
              w[:, c[6]:c[7]],
              w[:, c[7]:c[8]],
              w[:, c[8]:c[9]],
              w[:, c[9]:c[13]],
              w[:, c[14]:c[16]]]
    return [g.astype(BF16) for g in groups]


GATE_COL0 = 2 * GDN_HEADS


def kernel(x, mem, attn_norm_g, w_in, gdn_conv_w, gdn_a_log, gdn_dt_bias, gdn_norm_g, cmp_pe_k, cmp_w1_k, cmp_w2_k, cmp_pe_v, cmp_w1_v, cmp_w2_v, w_branch_a, w_branch_b, w_mix_out, xattn_norm_g, mem_norm_g, xattn_w_q, xattn_w_kv, xattn_w_o, ffn_norm_g, router_w, router_b, w_gate_up, b_gate_up, w_down, b_down, final_norm_g):
    bsz, seq, d = x.shape
    n = bsz * seq
    depth = w_in.shape[0]
    x2d = x.reshape(n, d)
    for l in range(depth):
        gqkv, gz, small, nq, nkc, nvc, nkv, mab = _inproj(x2d, attn_norm_g[l], _inproj_weights(w_in[l]), [F32] * 8)
        b3 = lambda a: a.reshape(bsz, seq, a.shape[-1])
        o_a = _gdn(b3(gqkv), b3(gz), b3(small), gdn_conv_w[l], gdn_a_log[l], gdn_dt_bias[l], gdn_norm_g[l], bsz, seq)
        q_bf, qr_bf, kse, vsa, kw, vwa = _nsa_prep(nq, nkv, seq)
        kc = _compress(b3(nkc), cmp_pe_k[l], cmp_w1_k[l], cmp_w2_k[l])
        vc = _compress(b3(nvc), cmp_pe_v[l], cmp_w1_v[l], cmp_w2_v[l])
        oc, nind = _nsa_select(b3(q_bf), kc, vc, bsz, seq)
        o_b = _nsa_attention(b3(qr_bf), nind, oc, b3(kse), b3(vsa), b3(kw), b3(vwa), b3(small), bsz, seq, GATE_COL0)
        memkv = _mem_kv(mem, mem_norm_g[l], xattn_w_kv[l])
        x2, h3, route, cnt = _post_mixer(o_a.reshape(n, -1), o_b.reshape(n, -1), mab, x2d, w_branch_a[l],
                                         w_branch_b[l], w_mix_out[l], xattn_norm_g[l], xattn_w_q[l], memkv,
                                         xattn_w_o[l], ffn_norm_g[l], router_w[l], router_b[l], seq)
        x2d = _moe(h3, x2, route, cnt, w_gate_up[l], b_gate_up[l], w_down[l], b_down[l], final_norm_g,
                   final_norm=(l == depth - 1))
    return x2d.reshape(bsz, seq, d)
```

```python
import functools
import math

import jax
import jax.numpy as jnp
import numpy as np
from jax import lax
from jax.experimental import pallas as pl
from jax.experimental.pallas import tpu as pltpu

F32 = jnp.float32
BF16 = jnp.bfloat16
HIGHEST = lax.Precision.HIGHEST

D_MODEL = 1024
NORM_EPS = 1e-6
L2_EPS = 1e-6
ROPE_THETA = 500000.0
NEG_INF = -1e30

GDN_HEADS = 8
GDN_DK = 128
GDN_DV = 128
GDN_CONV = 4
GDN_CHUNK = 64

NSA_HEADS = 8
NSA_GROUPS = 2
NSA_REP = NSA_HEADS // NSA_GROUPS
NSA_DK = 128
NSA_DV = 128
ROT_DIM = NSA_DK // 4
CMP_LEN = 32
CMP_STRIDE = 16
SEL_LEN = 64
SEL_TOPK = 16
WINDOW = 512
Q_BLOCK = 128
SEL_FORCE = 1e4

MEM_LEN = 256
XATTN_HEADS = 4
XATTN_DH = 128

N_EXPERTS = 32
TOP_K = 4
SWIGLU_LIMIT = 7.0
SWIGLU_ALPHA = 1.702
MOE_BLOCK = 256

LANES = 128
VMEM_LIMIT = 56 * 1024 * 1024
SEL_TILE = 512


def _cparams(n_axes, vmem=VMEM_LIMIT):
    return pltpu.CompilerParams(dimension_semantics=("arbitrary",) * n_axes, vmem_limit_bytes=vmem)


def _dot(a, b, precision=None):
    return jnp.dot(a, b, preferred_element_type=F32, precision=precision)


def _dot_nt(a, b, precision=None):
    return lax.dot_general(a, b, (((1,), (1,)), ((), ())), preferred_element_type=F32, precision=precision)


def _sigmoid(x):
    return 1.0 / (1.0 + jnp.exp(-x))


def _iota(shape, dim):
    return lax.broadcasted_iota(jnp.int32, shape, dim)


def _inproj_kernel(x_ref, g_ref, *refs):
    n = len(refs) // 2
    w_refs, o_refs = refs[:n], refs[n:]
    x = x_ref[...]
    ms = jnp.mean(x * x, axis=-1, keepdims=True)
    h = (x * lax.rsqrt(ms + NORM_EPS) * g_ref[...]).astype(BF16)
    for w_ref, o_ref in zip(w_refs, o_refs):
        o_ref[...] = _dot(h, w_ref[...]).astype(o_ref.dtype)


def _inproj(x2d, gain, weights, out_dtypes, tm=256):
    n, d = x2d.shape
    in_specs = [pl.BlockSpec((tm, d), lambda i: (i, 0)), pl.BlockSpec((1, d), lambda i: (0, 0))]
    in_specs += [pl.BlockSpec(w.shape, lambda i: (0, 0), pipeline_mode=pl.Buffered(1)) for w in weights]
    out_specs = [pl.BlockSpec((tm, w.shape[1]), lambda i: (i, 0)) for w in weights]
    out_shape = [jax.ShapeDtypeStruct((n, w.shape[1]), dt) for w, dt in zip(weights, out_dtypes)]
    return pl.pallas_call(
        _inproj_kernel, grid=(n // tm,), in_specs=in_specs, out_specs=out_specs, out_shape=out_shape,
        compiler_params=_cparams(1), name="inproj")(x2d, gain.reshape(1, d), *weights)


def _rope_tables(seq):
    half = ROT_DIM // 2
    inv_freq = jnp.exp(-math.log(ROPE_THETA) * jnp.arange(half, dtype=F32) * (2.0 / ROT_DIM))
    ang = jnp.arange(seq).astype(F32)[:, None] * inv_freq[None, :]
    cos, sin = jnp.cos(ang), jnp.sin(ang)
    ones = jnp.ones((seq, LANES - ROT_DIM), F32)
    zeros = jnp.zeros((seq, LANES - ROT_DIM), F32)
    zh = jnp.zeros((seq, half), F32)
    c = jnp.concatenate([cos, cos, ones], axis=1)
    sa = jnp.concatenate([zh, sin, zeros], axis=1)
    sb = jnp.concatenate([-sin, zh, zeros], axis=1)
    return c, sa, sb


def _rope(xh, c, sa, sb):
    half = ROT_DIM // 2
    return xh * c + pltpu.roll(xh, half, 1) * sa + pltpu.roll(xh, LANES - half, 1) * sb


MASK_BIG = 2.0 ** 100
Q_PRESCALE = NSA_DK ** -0.5 * math.log2(math.e)


def _nsa_prep_kernel(nq_ref, nkv_ref, c_ref, sa_ref, sb_ref, q_ref, qr_ref, kse_ref, vsa_ref, kw_ref, vwa_ref,
                     *, rows_per_seq):
    tm = nq_ref.shape[0]
    c, sa, sb = c_ref[...], sa_ref[...], sb_ref[...]
    for h in range(NSA_HEADS):
        sl = slice(h * NSA_DK, (h + 1) * NSA_DK)
        xh = nq_ref[:, sl] * Q_PRESCALE
        q_ref[:, sl] = xh.astype(BF16)
        qr_ref[:, sl] = _rope(xh, c, sa, sb).astype(BF16)
    t0 = (pl.program_id(0) % rows_per_seq) * tm
    key_blk = jnp.right_shift(t0 + _iota((tm, 1), 0), int(math.log2(SEL_LEN)))
    blk_neg = jnp.where(key_blk == _iota((1, LANES), 1), -MASK_BIG, 0.0).astype(BF16)
    ones = jnp.ones((tm, LANES), BF16)
    for g in range(NSA_GROUPS):
        col = lambda j: nkv_ref[:, (j * NSA_GROUPS + g) * LANES:(j * NSA_GROUPS + g + 1) * LANES]
        wide = slice(2 * g * LANES, (2 * g + 1) * LANES)
        aux = slice((2 * g + 1) * LANES, (2 * g + 2) * LANES)
        kse_ref[:, wide] = _rope(col(0), c, sa, sb).astype(BF16)
        kse_ref[:, aux] = blk_neg
        vsa_ref[:, wide] = col(1).astype(BF16)
        vsa_ref[:, aux] = ones
        kw_ref[:, g * LANES:(g + 1) * LANES] = _rope(col(2), c, sa, sb).astype(BF16)
        vwa_ref[:, wide] = col(3).astype(BF16)
        vwa_ref[:, aux] = ones


def _nsa_prep(nq, nkv, seq, tm=512):
    n = nq.shape[0]
    assert seq // SEL_LEN <= LANES
    c, sa, sb = _rope_tables(seq)
    nt = seq // tm
    g = NSA_GROUPS
    row = lambda w: pl.BlockSpec((tm, w), lambda i: (i, 0))
    tab = pl.BlockSpec((tm, LANES), lambda i: (i % nt, 0))
    out_w = [nq.shape[1], nq.shape[1], 2 * g * LANES, 2 * g * LANES, g * LANES, 2 * g * LANES]
    return pl.pallas_call(
        functools.partial(_nsa_prep_kernel, rows_per_seq=nt), grid=(n // tm,),
        in_specs=[row(nq.shape[1]), row(nkv.shape[1]), tab, tab, tab],
        out_specs=[row(w) for w in out_w],
        out_shape=[jax.ShapeDtypeStruct((n, w), BF16) for w in out_w],
        compiler_params=_cparams(1), name="nsa_prep")(nq, nkv, c, sa, sb)


def _compress_kernel(x_ref, pea_ref, peb_ref, w1a_ref, w1b_ref, w2_ref, o_ref):
    x = x_ref[0]
    y0 = _dot((x + pea_ref[...]).astype(BF16), w1a_ref[...])
    y1 = _dot((x + peb_ref[...]).astype(BF16), w1b_ref[...])
    n = y1.shape[0]
    pre = y0 + pltpu.roll(y1, n - 1, 0)
    act = pre * _sigmoid(pre)
    o_ref[0] = _dot(act.astype(BF16), w2_ref[...]).astype(o_ref.dtype)


def _compress(x, pe, w1, w2):
    bsz, seq, gd = x.shape
    g = NSA_GROUPS
    d = gd // g
    half = CMP_LEN // 2
    assert CMP_STRIDE == half
    xr = x.reshape(bsz, seq // half, half * gd)
    eye = jnp.eye(g, dtype=F32)

    def expand_w1(w):
        return jnp.einsum("ldk,gh->lgdhk", w.reshape(half, d, d), eye).reshape(half * gd, gd).astype(BF16)

    w1a, w1b = expand_w1(w1[: half * d]), expand_w1(w1[half * d:])
    w2b = jnp.einsum("dk,gh->gdhk", w2, eye).reshape(gd, gd).astype(BF16)
    pea = jnp.broadcast_to(pe[:half, None, :], (half, g, d)).reshape(1, half * gd)
    peb = jnp.broadcast_to(pe[half:, None, :], (half, g, d)).reshape(1, half * gd)
    nrow = seq // half
    full = lambda a: pl.BlockSpec(a.shape, lambda b: (0,) * a.ndim)
    return pl.pallas_call(
        _compress_kernel, grid=(bsz,),
        in_specs=[pl.BlockSpec((1, nrow, half * gd), lambda b: (b, 0, 0)),
                  full(pea), full(peb), full(w1a), full(w1b), full(w2b)],
        out_specs=pl.BlockSpec((1, nrow, gd), lambda b: (b, 0, 0)),
        out_shape=jax.ShapeDtypeStruct((bsz, nrow, gd), BF16),
        compiler_params=_cparams(1), name="nsa_compress")(xr, pea, peb, w1a, w1b, w2b)


def _sel_wmap(seq):
    n_cmp = (seq - CMP_LEN) // CMP_STRIDE + 1
    n_sel = seq // SEL_LEN
    r_s = SEL_LEN // CMP_STRIDE
    r_c = CMP_LEN // CMP_STRIDE
    w = np.zeros((seq // CMP_STRIDE, n_sel), np.float32)
    for s in range(n_sel):
        for m in range(r_s):
            for nn in range(r_c):
                c = s * r_s + m - nn
                if 0 <= c < n_cmp:
                    w[c, s] += 1.0
    return w


def _heads_to_rows(ref, r0):
    return jnp.concatenate([ref[0, pl.ds(r0, Q_BLOCK), r * NSA_DK:(r + 1) * NSA_DK] for r in range(NSA_REP)], axis=0)


def _split3(x):
    hi = x.astype(BF16)
    r1 = x - hi.astype(F32)
    mid = r1.astype(BF16)
    lo = (r1 - mid.astype(F32)).astype(BF16)
    return hi, mid, lo


SELECT_QBLOCKS = 8


def _nsa_select_kernel(q_ref, kc_ref, vc_ref, wmap_ref, oc_ref, nind_ref, imp_ref, *, seq):
    j = pl.program_id(2)
    rows = NSA_REP * Q_BLOCK
    n_sel = seq // SEL_LEN
    n_cmp_pad = seq // CMP_STRIDE
    n_q = SELECT_QBLOCKS * Q_BLOCK
    base = j * n_q
    kc = kc_ref[0]
    vc = vc_ref[0]
    cmp_end = _iota((1, n_cmp_pad), 1) * CMP_STRIDE + (CMP_LEN - 1)

    def per_block(qb_i, carry):
        r0 = pl.multiple_of(qb_i * Q_BLOCK, Q_BLOCK)
        tq1 = base + r0 + _iota((Q_BLOCK, 1), 0)
        s = _dot_nt(_heads_to_rows(q_ref, r0), kc).reshape(NSA_REP, Q_BLOCK, n_cmp_pad)
        mask = (cmp_end <= tq1)[None]
        s = jnp.where(mask, s, NEG_INF)
        p = jnp.where(mask, jnp.exp2(s - jnp.max(s, axis=-1, keepdims=True)), 0.0)
        denom = jnp.maximum(jnp.sum(p, axis=-1, keepdims=True), 1e-30)
        p_c = p / denom
        o_c = _dot(p_c.reshape(rows, n_cmp_pad).astype(BF16), vc)
        for r in range(NSA_REP):
            oc_ref[0, pl.ds(r0, Q_BLOCK), r * NSA_DV:(r + 1) * NSA_DV] = o_c[r * Q_BLOCK:(r + 1) * Q_BLOCK]
        p_sum = p_c[0]
        for r in range(1, NSA_REP):
            p_sum = p_sum + p_c[r]
        hi, mid, lo = _split3(p_sum)
        w = wmap_ref[...]
        imp_ref[pl.ds(r0, Q_BLOCK), :] = _dot(hi, w) + _dot(mid, w) + _dot(lo, w)
        return carry

    lax.fori_loop(0, SELECT_QBLOCKS, per_block, 0)

    n_pick = SEL_TOPK - 3
    tq = base + _iota((n_q, 1), 0)
    sel_ids = _iota((1, LANES), 1)
    cur = jnp.right_shift(tq, int(math.log2(SEL_LEN)))
    forced = (sel_ids == 0) | (sel_ids == cur) | (sel_ids == cur - 1)
    causal_blk = sel_ids * SEL_LEN <= tq
    cand = jnp.where(causal_blk & jnp.logical_not(forced), imp_ref[...], -jnp.inf)

    def write(picked):
        keep = forced | (picked > 0.5)
        nind_ref[0, 0] = jnp.where(causal_blk & keep, 0.0, 1.0).astype(BF16)

    score = cand
    picked = jnp.zeros((n_q, LANES), F32)
    for _ in range(n_pick):
        mx = jnp.max(score, axis=-1, keepdims=True)
        pick = (score == mx) & (mx > -jnp.inf)
        picked = jnp.where(pick, 1.0, picked)
        score = jnp.where(pick, -jnp.inf, score)
    write(picked)
    most = jnp.max(jnp.sum(picked, axis=-1, keepdims=True))

    @pl.when(most > n_pick)
    def _():
        lane = _iota((n_q, LANES), 1).astype(F32)
        score = cand
        picked = jnp.zeros((n_q, LANES), F32)
        for _ in range(n_pick):
            mx = jnp.max(score, axis=-1, keepdims=True)
            hit = (score == mx) & (mx > -jnp.inf)
            first = jnp.min(jnp.where(hit, lane, float(LANES)), axis=-1, keepdims=True)
            pick = lane == first
            picked = jnp.where(pick, 1.0, picked)
            score = jnp.where(pick, -jnp.inf, score)
        write(picked)


def _nsa_select(q, kc, vc, bsz, seq):
    g = NSA_GROUPS
    n_q = SELECT_QBLOCKS * Q_BLOCK
    wmap = jnp.asarray(_sel_wmap(seq), BF16)
    wmap = jnp.pad(wmap, ((0, 0), (0, LANES - wmap.shape[1])))
    cspec = pl.BlockSpec((1, seq // CMP_STRIDE, LANES), lambda b, gg, j: (b, 0, gg))
    kernel = functools.partial(_nsa_select_kernel, seq=seq)
    return pl.pallas_call(
        kernel, grid=(bsz, g, seq // n_q),
        in_specs=[pl.BlockSpec((1, n_q, NSA_REP * NSA_DK), lambda b, gg, j: (b, j, gg)), cspec, cspec,
                  pl.BlockSpec(wmap.shape, lambda b, gg, j: (0, 0))],
        out_specs=[pl.BlockSpec((1, n_q, NSA_REP * NSA_DV), lambda b, gg, j: (b, j, gg)),
                   pl.BlockSpec((1, 1, n_q, LANES), lambda b, gg, j: (b, gg, j, 0))],
        out_shape=[jax.ShapeDtypeStruct((bsz, seq, NSA_HEADS * NSA_DV), F32),
                   jax.ShapeDtypeStruct((bsz, g, seq, LANES), BF16)],
        scratch_shapes=[pltpu.VMEM((n_q, LANES), F32)],
        compiler_params=_cparams(3), name="nsa_select")(q, kc, vc, wmap)


def _nsa_attn_kernel(qr_ref, nind_ref, oc_ref, kse_ref, vsa_ref, kw_ref, vwa_ref, gate_ref, o_ref, m_ref, acc_ref,
                     sa_ref, sb_ref, *, gate_col0):
    g = pl.program_id(1)
    i = pl.program_id(2)
    q0 = i * Q_BLOCK
    rows = NSA_REP * Q_BLOCK
    tq1 = q0 + _iota((Q_BLOCK, 1), 0)

    qrb = _heads_to_rows(qr_ref, 0)
    lhs = jnp.concatenate([qrb, jnp.concatenate([nind_ref[0, 0]] * NSA_REP, axis=0)], axis=1)

    m_ref[...] = jnp.full(m_ref.shape, -MASK_BIG, F32)
    acc_ref[...] = jnp.zeros(acc_ref.shape, F32)

    n_tiles = q0 // SEL_TILE + 1

    def tile_start(t):
        return pl.multiple_of(jnp.minimum(t, n_tiles - 1) * SEL_TILE, SEL_TILE)

    def scores(t):
        return _dot_nt(lhs, kse_ref[0, pl.ds(tile_start(t), SEL_TILE), :])

    def tile_terms(s, t):
        k0 = tile_start(t)
        ok = (k0 + _iota((1, SEL_TILE), 1) <= tq1) & (t < n_tiles)
        bias = jnp.where(ok, 0.0, -MASK_BIG)
        s = (s.reshape(NSA_REP, Q_BLOCK, SEL_TILE) + bias[None]).reshape(rows, SEL_TILE)
        m_t = jnp.max(s, axis=-1, keepdims=True)
        p = jnp.exp2(s - m_t).astype(BF16)
        return m_t, _dot(p, vsa_ref[0, pl.ds(k0, SEL_TILE), :])

    def merge(terms):
        m_new = m_ref[...]
        for m_t, _ in terms:
            m_new = jnp.maximum(m_new, m_t)
        acc = jnp.exp2(m_ref[...] - m_new) * acc_ref[...]
        for m_t, pv in terms:
            acc = acc + jnp.exp2(m_t - m_new) * pv
        acc_ref[...] = acc
        m_ref[...] = m_new

    sa_ref[...] = scores(0)

    def tile_pair(p, carry):
        t0 = 2 * p
        sb_ref[...] = scores(t0 + 1)
        first = tile_terms(sa_ref[...], t0)
        sa_ref[...] = scores(t0 + 2)
        merge([first, tile_terms(sb_ref[...], t0 + 1)])
        return carry

    lax.fori_loop(0, (n_tiles + 1) // 2, tile_pair, 0)
    o_s = acc_ref[:, :NSA_DV] / acc_ref[:, NSA_DV:]

    wlen = WINDOW + Q_BLOCK
    w0 = pl.multiple_of(jnp.maximum(q0 - WINDOW, 0), Q_BLOCK)
    s_w = _dot_nt(qrb, kw_ref[0, pl.ds(w0, wlen), :]).reshape(NSA_REP, Q_BLOCK, wlen)
    dlt = tq1 - (w0 + _iota((1, wlen), 1))
    band = jnp.where((dlt >= 0) & (dlt < WINDOW), 0.0, -MASK_BIG)
    s_w = (s_w + band[None]).reshape(rows, wlen)
    p_w = jnp.exp2(s_w - jnp.max(s_w, axis=-1, keepdims=True)).astype(BF16)
    pv_w = _dot(p_w, vwa_ref[0, pl.ds(w0, wlen), :])
    o_w = pv_w[:, :NSA_DV] / pv_w[:, NSA_DV:]

    gts = _sigmoid(gate_ref[0])
    for r in range(NSA_REP):
        rs = slice(r * Q_BLOCK, (r + 1) * Q_BLOCK)
        cols = []
        for j in range(3):
            col = gate_col0 + (g * NSA_REP + r) * 3 + j
            onehot = _iota((1, LANES), 1) == col
            cols.append(jnp.sum(jnp.where(onehot, gts, 0.0), axis=-1, keepdims=True))
        o_c = oc_ref[0, :, r * NSA_DV:(r + 1) * NSA_DV]
        out = cols[0] * o_c + cols[1] * o_s[rs] + cols[2] * o_w[rs]
        o_ref[0, :, r * NSA_DV:(r + 1) * NSA_DV] = out.astype(o_ref.dtype)


def _nsa_attention(qr, nind, oc, kse, vsa, kw, vwa, small, bsz, seq, gate_col0):
    g = NSA_GROUPS
    rows = NSA_REP * Q_BLOCK
    qspec = pl.BlockSpec((1, Q_BLOCK, NSA_REP * NSA_DK), lambda b, gg, i: (b, i, gg))
    wide = pl.BlockSpec((1, seq, 2 * LANES), lambda b, gg, i: (b, 0, gg))
    kernel = functools.partial(_nsa_attn_kernel, gate_col0=gate_col0)
    return pl.pallas_call(
        kernel, grid=(bsz, g, seq // Q_BLOCK),
        in_specs=[qspec,
                  pl.BlockSpec((1, 1, Q_BLOCK, LANES), lambda b, gg, i: (b, gg, i, 0)),
                  qspec, wide, wide,
                  pl.BlockSpec((1, seq, LANES), lambda b, gg, i: (b, 0, gg)),
                  wide,
                  pl.BlockSpec((1, Q_BLOCK, LANES), lambda b, gg, i: (b, i, 0))],
        out_specs=qspec,
        out_shape=jax.ShapeDtypeStruct((bsz, seq, NSA_HEADS * NSA_DV), BF16),
        scratch_shapes=[pltpu.VMEM((rows, 1), F32), pltpu.VMEM((rows, 2 * NSA_DV), F32),
                        pltpu.VMEM((rows, SEL_TILE), F32), pltpu.VMEM((rows, SEL_TILE), F32)],
        compiler_params=_cparams(3), name="nsa_attn")(qr, nind, oc, kse, vsa, kw, vwa, small)


GDN_UNIT = 2 * GDN_CHUNK
CONV_HALO = 8


def _lcat(a, b):
    return jnp.concatenate([a, b], axis=1)


def _bdiag(a, b):
    za = jnp.zeros((a.shape[0], b.shape[1]), a.dtype)
    zb = jnp.zeros((b.shape[0], a.shape[1]), b.dtype)
    return jnp.concatenate([_lcat(a, za), _lcat(zb, b)], axis=0)


def _hilo(x):
    hi = x.astype(BF16)
    return hi, (x - hi.astype(F32)).astype(BF16)


def _pair_rhs(x):
    m = x.shape[1] // 2
    hi, lo = _hilo(x)
    return _bdiag(hi[:, :m], hi[:, m:]), _bdiag(lo[:, :m], lo[:, m:])


def _mm3(lhs, rhs):
    return _dot(lhs[0], rhs[0]) + _dot(lhs[0], rhs[1]) + _dot(lhs[1], rhs[0])


def _pair_lower_inverse(mats, eye, b16, b32):
    sq = lambda xs: [_mm3(_hilo(x), _pair_rhs(x)) for x in xs]
    mul = lambda xs, ys: [_mm3(_hilo(x), _pair_rhs(y)) for x, y in zip(xs, ys)]
    d = [jnp.where(b16, a, 0.0) for a in mats]
    d2 = sq(d)
    d4 = sq(d2)
    d8 = sq(d4)
    t = mul([eye - x for x in d], [eye + x for x in d2])
    t = mul(t, [eye + x for x in d4])
    t = mul(t, [eye + x for x in d8])
    for pick in (lambda a: jnp.where(b32 & jnp.logical_not(b16), a, 0.0), lambda a: jnp.where(b32, 0.0, a)):
        inner = mul([pick(a) for a in mats], t)
        t = [x - y for x, y in zip(t, mul(t, inner))]
    return t


def _gdn_kernel(x_ref, halo_ref, z_ref, small_ref, convw_ref, alog_ref, dtb_ref, ng_ref, o_ref, s_ref):
    i = pl.program_id(1)
    u_len = GDN_UNIT
    c_len = GDN_CHUNK
    dk, dv = GDN_DK, GDN_DV
    hq = GDN_HEADS * dk

    @pl.when(i == 0)
    def _():
        s_ref[...] = jnp.zeros(s_ref.shape, F32)

    row = _iota((u_len, 2 * u_len), 0)
    col = _iota((u_len, 2 * u_len), 1) & (u_len - 1)
    same = jnp.right_shift(row, 6) == jnp.right_shift(col, 6)
    incl = same & (row >= col)
    strict = same & (row > col)
    b16 = jnp.right_shift(row, 4) == jnp.right_shift(col, 4)
    b32 = jnp.right_shift(row, 5) == jnp.right_shift(col, 5)
    eye = (row == col).astype(F32)
    eye1 = eye[:, :u_len] > 0.5
    rcol = _iota((u_len, 1), 0)
    lane = _iota((1, LANES), 1)
    first_rows = rcol < c_len
    first_cols = (_iota((1, 2 * u_len), 1) & (u_len - 1)) < c_len
    head0 = _iota((1, 2 * dv), 1) < dv
    halo_on = (i > 0).astype(F32)

    sm = small_ref[0]
    beta_all = _sigmoid(sm)
    xs = sm + dtb_ref[...]
    softplus = jnp.maximum(xs, 0.0) + jnp.log1p(jnp.exp(-jnp.abs(xs)))
    ld_all = -jnp.exp(alog_ref[...]) * softplus
    cum = incl[:, :u_len].astype(BF16)
    ld_hi, ld_mid, ld_lo = _split3(ld_all)
    gc_all = _dot(cum, ld_hi) + _dot(cum, ld_mid) + _dot(cum, ld_lo)

    def pick_col(a, c):
        return jnp.sum(jnp.where(lane == c, a, 0.0), axis=-1, keepdims=True)

    def conv_silu(c0):
        xf = jnp.concatenate([halo_ref[0, :, c0:c0 + LANES] * halo_on, x_ref[0, :, c0:c0 + LANES]], axis=0)
        w = convw_ref[:, c0:c0 + LANES]
        y = w[0:1] * xf[CONV_HALO - 3:CONV_HALO - 3 + u_len]
        for j in range(1, GDN_CONV):
            off = CONV_HALO - (GDN_CONV - 1) + j
            y = y + w[j:j + 1] * xf[off:off + u_len]
        return y * _sigmoid(y)

    def head_terms(h):
        q = conv_silu(h * dk)
        k = conv_silu(hq + h * dk)
        v = conv_silu(2 * hq + h * dv)
        q = q * lax.rsqrt(jnp.sum(q * q, axis=-1, keepdims=True) + L2_EPS) * (dk ** -0.5)
        k = k * lax.rsqrt(jnp.sum(k * k, axis=-1, keepdims=True) + L2_EPS)
        beta = pick_col(beta_all, h)
        gcol = pick_col(gc_all, GDN_HEADS + h)
        grow = jnp.sum(jnp.where(eye1, gcol, 0.0), axis=0, keepdims=True)
        gl0 = jnp.sum(jnp.where(rcol == c_len - 1, gcol, 0.0), axis=0, keepdims=True)
        gl1 = jnp.sum(jnp.where(rcol == u_len - 1, gcol, 0.0), axis=0, keepdims=True)
        e_g = jnp.exp(gcol)
        kb = k * beta
        return dict(q=q, kb=kb, k_t=k.T, vb_kw=_lcat(v * beta, kb * e_g), qd=q * e_g,
                    diff=gcol - grow, kd_scale=jnp.exp(jnp.where(first_cols[:, :u_len], gl0, gl1) - grow),
                    g0=jnp.exp(gl0), g1=jnp.exp(gl1))

    n_pairs = GDN_HEADS // 2
    prs = range(n_pairs)
    heads = [head_terms(h) for h in range(GDN_HEADS)]
    pa = [heads[2 * p] for p in prs]
    pb = [heads[2 * p + 1] for p in prs]
    decay = [jnp.where(incl, jnp.exp(jnp.where(incl, _lcat(a["diff"], b["diff"]), 0.0)), 0.0)
             for a, b in zip(pa, pb)]
    kt_rhs = [_pair_rhs(_lcat(a["k_t"], b["k_t"])) for a, b in zip(pa, pb)]
    gram = [_mm3(_hilo(_lcat(a["kb"], b["kb"])), r) for a, b, r in zip(pa, pb, kt_rhs)]
    t_mat = _pair_lower_inverse([jnp.where(strict, g_ * d_, 0.0) for g_, d_ in zip(gram, decay)], eye, b16, b32)
    vk = [(_hilo(a["vb_kw"]), _hilo(b["vb_kw"])) for a, b in zip(pa, pb)]
    uw = [_mm3(_hilo(t), (_bdiag(va[0], vb[0]), _bdiag(va[1], vb[1])))
          for t, (va, vb) in zip(t_mat, vk)]
    u = [_lcat(x[:, :dv], x[:, dv + dk:2 * dv + dk]) for x in uw]
    w = [_lcat(x[:, dv:dv + dk], x[:, 2 * dv + dk:]).astype(BF16) for x in uw]
    qk = [jnp.where(incl, _dot(_lcat(a["q"], b["q"]).astype(BF16), r[0]) * d_, 0.0).astype(BF16)
          for a, b, r, d_ in zip(pa, pb, kt_rhs, decay)]
    w_qd = [jnp.concatenate([w_, _lcat(a["qd"], b["qd"]).astype(BF16)], axis=0)
            for w_, a, b in zip(w, pa, pb)]
    kd = [_lcat(a["k_t"] * a["kd_scale"], b["k_t"] * b["kd_scale"]) for a, b in zip(pa, pb)]
    qk_kd = [[jnp.concatenate([qk_, jnp.where(first_cols, x, 0.0).astype(BF16)], axis=0) for qk_, x in zip(qk, kd)],
             [jnp.concatenate([qk_, jnp.where(first_cols, 0.0, x).astype(BF16)], axis=0) for qk_, x in zip(qk, kd)]]

    s = [s_ref[p] for p in prs]
    outs = []
    for c in range(2):
        gkey = "g%d" % c
        sb = [x.astype(BF16) for x in s]
        r1 = [_dot(w_qd[p], _bdiag(sb[p][:, :dv], sb[p][:, dv:])) for p in prs]
        vn = [(u[p] - r1[p][:u_len]).astype(BF16) for p in prs]
        r2 = [_dot(qk_kd[c][p], _bdiag(vn[p][:, :dv], vn[p][:, dv:])) for p in prs]
        outs.append([r1[p][u_len:] + r2[p][:u_len] for p in prs])
        s = [s[p] * jnp.where(head0, pa[p][gkey], pb[p][gkey]) + r2[p][u_len:] for p in prs]
    for p in prs:
        s_ref[p] = s[p]
        o_pair = jnp.where(first_rows, outs[0][p], outs[1][p])
        for j in range(2):
            h = 2 * p + j
            o = o_pair[:, j * dv:(j + 1) * dv]
            o = o * lax.rsqrt(jnp.mean(o * o, axis=-1, keepdims=True) + NORM_EPS) * ng_ref[...]
            z = z_ref[0, :, h * dv:(h + 1) * dv]
            o_ref[0, :, h * dv:(h + 1) * dv] = (o * (z * _sigmoid(z))).astype(o_ref.dtype)


def _gdn(gqkv, gz, small, conv_w, a_log, dt_bias, norm_g, bsz, seq):
    u_len = GDN_UNIT
    c_all = gqkv.shape[-1]
    hv = GDN_HEADS * GDN_DV
    pad = jnp.zeros((GDN_HEADS,), F32)
    tail = jnp.zeros((LANES - 2 * GDN_HEADS,), F32)
    alog_row = jnp.concatenate([pad, a_log.astype(F32), tail]).reshape(1, LANES)
    dtb_row = jnp.concatenate([pad, dt_bias.astype(F32), tail]).reshape(1, LANES)
    per_halo = u_len // CONV_HALO
    return pl.pallas_call(
        _gdn_kernel, grid=(bsz, seq // u_len),
        in_specs=[pl.BlockSpec((1, u_len, c_all), lambda b, i: (b, i, 0)),
                  pl.BlockSpec((1, CONV_HALO, c_all), lambda b, i: (b, jnp.maximum(i * per_halo - 1, 0), 0)),
                  pl.BlockSpec((1, u_len, hv), lambda b, i: (b, i, 0)),
                  pl.BlockSpec((1, u_len, LANES), lambda b, i: (b, i, 0)),
                  pl.BlockSpec(conv_w.shape, lambda b, i: (0, 0)),
                  pl.BlockSpec((1, LANES), lambda b, i: (0, 0)),
                  pl.BlockSpec((1, LANES), lambda b, i: (0, 0)),
                  pl.BlockSpec((1, GDN_DV), lambda b, i: (0, 0))],
        out_specs=pl.BlockSpec((1, u_len, hv), lambda b, i: (b, i, 0)),
        out_shape=jax.ShapeDtypeStruct((bsz, seq, hv), BF16),
        scratch_shapes=[pltpu.VMEM((GDN_HEADS // 2, GDN_DK, 2 * GDN_DV), F32)],
        compiler_params=_cparams(2), name="gdn")(gqkv, gqkv, gz, small, conv_w, alog_row, dtb_row,
                                                  norm_g.reshape(1, GDN_DV))


def _rms(x, gain):
    return x * lax.rsqrt(jnp.mean(x * x, axis=-1, keepdims=True) + NORM_EPS) * gain


def _mem_kv_kernel(m_ref, g_ref, w_ref, o_ref):
    h = _rms(m_ref[0], g_ref[...]).astype(BF16)
    o_ref[0] = _dot(h, w_ref[...]).astype(o_ref.dtype)


def _mem_kv(mem, gain, w_kv):
    bsz, m_len, d = mem.shape
    w = w_kv.astype(BF16)
    return pl.pallas_call(
        _mem_kv_kernel, grid=(bsz,),
        in_specs=[pl.BlockSpec((1, m_len, d), lambda b: (b, 0, 0)), pl.BlockSpec((1, d), lambda b: (0, 0)),
                  pl.BlockSpec(w.shape, lambda b: (0, 0))],
        out_specs=pl.BlockSpec((1, m_len, w.shape[1]), lambda b: (b, 0, 0)),
        out_shape=jax.ShapeDtypeStruct((bsz, m_len, w.shape[1]), BF16),
        compiler_params=_cparams(1), name="mem_kv")(mem, gain.reshape(1, d), w)


ROUTE_E, ROUTE_G, ROUTE_R = 0, TOP_K, 2 * TOP_K


def _post_mixer_kernel(oa_ref, ob_ref, mab_ref, x_ref, wa_ref, wb_ref, wmix_ref, gx_ref, wq_ref, km_ref, vm_ref,
                       wo_ref, gf_ref, rw_ref, rb_ref, x2_ref, h3_ref, route_ref, cnt_ref, run_ref):
    i = pl.program_id(0)
    tm, d = x_ref.shape
    ts = POST_SUB
    subs = [slice(j * ts, (j + 1) * ts) for j in range(tm // ts)]

    @pl.when(i == 0)
    def _():
        run_ref[...] = jnp.zeros(run_ref.shape, F32)

    ya = [_dot(oa_ref[r, :], wa_ref[...]) for r in subs]
    yb = [_dot(ob_ref[r, :], wb_ref[...]) for r in subs]
    mixed = [(_sigmoid(mab_ref[r, :d]) * a + _sigmoid(mab_ref[r, d:]) * b).astype(BF16) for r, a, b in zip(subs, ya, yb)]
    x1 = [x_ref[r, :] + _dot(m, wmix_ref[...]) for r, m in zip(subs, mixed)]

    q = [_dot(_rms(x, gx_ref[...]).astype(BF16), wq_ref[...]).astype(BF16) for x in x1]
    heads = []
    for hd in range(XATTN_HEADS):
        sl = slice(hd * XATTN_DH, (hd + 1) * XATTN_DH)
        s = [_dot_nt(qq[:, sl], km_ref[0, :, sl]) * (XATTN_DH ** -0.5) for qq in q]
        p = [jnp.exp(x - jnp.max(x, axis=-1, keepdims=True)) for x in s]
        p = [(x / jnp.sum(x, axis=-1, keepdims=True)).astype(BF16) for x in p]
        heads.append([_dot(x, vm_ref[0, :, sl]) for x in p])
    o = [jnp.concatenate([heads[hd][j] for hd in range(XATTN_HEADS)], axis=-1).astype(BF16) for j in range(len(subs))]
    x2 = [x + _dot(oo, wo_ref[...]) for x, oo in zip(x1, o)]

    h3 = [_rms(x, gf_ref[...]) for x in x2]
    logits = []
    for h in h3:
        acc = rb_ref[...]
        for term in _split3(h):
            both = _dot(term, rw_ref[...])
            acc = acc + (both[:, :LANES] + both[:, LANES:])
        logits.append(acc)
    for r, x, h in zip(subs, x2, h3):
        x2_ref[r, :] = x
        h3_ref[r, :] = h

    lane = _iota((ts, LANES), 1).astype(F32)
    work = logits
    vals, idxs = [], []
    for _ in range(TOP_K):
        mx = [jnp.max(w, axis=-1, keepdims=True) for w in work]
        first = [jnp.min(jnp.where(w == m, lane, float(LANES)), axis=-1, keepdims=True) for w, m in zip(work, mx)]
        vals.append(mx)
        idxs.append(first)
        work = [jnp.where(lane == f, -jnp.inf, w) for w, f in zip(work, first)]
    earlier = (_iota((ts, ts), 0) > _iota((ts, ts), 1)).astype(BF16)
    run = run_ref[...]
    for j, r in enumerate(subs):
        exps = [jnp.exp(vals[k][j] - vals[0][j]) for k in range(TOP_K)]
        den = exps[0]
        for e in exps[1:]:
            den = den + e
        onehot = jnp.zeros((ts, LANES), F32)
        for k in range(TOP_K):
            onehot = onehot + (lane == idxs[k][j]).astype(F32)
        before = run + _dot(earlier, onehot.astype(BF16))
        route = jnp.zeros((ts, LANES), F32)
        for k in range(TOP_K):
            rank = jnp.sum(jnp.where(lane == idxs[k][j], before, 0.0), axis=-1, keepdims=True)
            route = jnp.where(lane == float(ROUTE_E + k), idxs[k][j], route)
            route = jnp.where(lane == float(ROUTE_G + k), exps[k] / den, route)
            route = jnp.where(lane == float(ROUTE_R + k), rank, route)
        route_ref[r, :] = route
        run = run + jnp.sum(onehot, axis=0, keepdims=True)
    run_ref[...] = run
    cnt_ref[...] = run


POST_SUB = 256


def _post_mixer(o_a, o_b, mab, x2d, w_a, w_b, w_mix, gx, w_q, memkv, w_o, gf, router_w, router_b, seq, tm=512):
    n, d = x2d.shape
    xd = w_q.shape[1]
    m_len = memkv.shape[1]
    per_b = seq // tm
    n_exp = router_w.shape[1]
    rw32 = jnp.concatenate([router_w.astype(F32), jnp.zeros((d, LANES - n_exp), F32)], axis=1)
    rw_hi = rw32.astype(BF16)
    rw = jnp.concatenate([rw_hi, (rw32 - rw_hi.astype(F32)).astype(BF16)], axis=1)
    rb = jnp.concatenate([router_b.astype(F32), jnp.full((LANES - n_exp,), NEG_INF, F32)]).reshape(1, LANES)
    row = lambda w: pl.BlockSpec((tm, w), lambda i: (i, 0))
    full = lambda a: pl.BlockSpec(a.shape, lambda i: (0,) * a.ndim)
    wa, wb, wm, wq, wo = (w.astype(BF16) for w in (w_a, w_b, w_mix, w_q, w_o))
    gx2, gf2 = gx.reshape(1, d), gf.reshape(1, d)
    return pl.pallas_call(
        _post_mixer_kernel, grid=(n // tm,),
        in_specs=[row(d), row(d), row(2 * d), row(d), full(wa), full(wb), full(wm), full(gx2), full(wq),
                  pl.BlockSpec((1, m_len, xd), lambda i: (i // per_b, 0, 0)),
                  pl.BlockSpec((1, m_len, xd), lambda i: (i // per_b, 0, 1)),
                  full(wo), full(gf2), full(rw), full(rb)],
        out_specs=[row(d), row(d), row(LANES), pl.BlockSpec((1, LANES), lambda i: (0, 0))],
        out_shape=[jax.ShapeDtypeStruct((n, d), F32), jax.ShapeDtypeStruct((n, d), F32),
                   jax.ShapeDtypeStruct((n, LANES), F32), jax.ShapeDtypeStruct((1, LANES), F32)],
        scratch_shapes=[pltpu.VMEM((1, LANES), F32)],
        compiler_params=_cparams(1), name="post_mixer")(o_a, o_b, mab, x2d, wa, wb, wm, gx2, wq, memkv, memkv, wo,
                                                        gf2, rw, rb)


MOE_DISPATCH_TILE = 512
MOE_COMBINE_TILE = 256


def _dispatch_kernel(dest_ref, h_ref, xs_in_ref, xs_ref, sem):
    del xs_in_ref
    tt = h_ref.shape[0]

    def row_copy(t, dst_row):
        return pltpu.make_async_copy(h_ref.at[pl.ds(t, 1)], xs_ref.at[pl.ds(dst_row, 1)], sem)

    def start(t, c):
        for k in range(TOP_K):
            row_copy(t, dest_ref[0, 0, t * TOP_K + k]).start(priority=k % 2)
        return c

    def wait(t, c):
        for k in range(TOP_K):
            row_copy(t, dest_ref[0, 0, t * TOP_K + k]).wait()
        return c

    lax.fori_loop(0, tt, start, 0)
    lax.fori_loop(0, tt, wait, 0)


def _dispatch(h3, dest, n_rows):
    n, d = h3.shape
    tt = MOE_DISPATCH_TILE
    dest3 = dest.reshape(n // tt, 1, tt * TOP_K)
    xs0 = jnp.zeros((n_rows, d), h3.dtype)
    return pl.pallas_call(
        _dispatch_kernel, grid=(n // tt,),
        in_specs=[pl.BlockSpec((1, 1, tt * TOP_K), lambda i: (i, 0, 0), memory_space=pltpu.SMEM),
                  pl.BlockSpec((tt, d), lambda i: (i, 0)),
                  pl.BlockSpec(memory_space=pl.ANY)],
        out_specs=pl.BlockSpec(memory_space=pl.ANY),
        out_shape=jax.ShapeDtypeStruct((n_rows, d), h3.dtype),
        scratch_shapes=[pltpu.SemaphoreType.DMA(())],
        input_output_aliases={2: 0},
        compiler_params=_cparams(1), name="moe_dispatch")(dest3, h3, xs0)


def _expert_kernel(be_ref, nu_ref, xs_ref, wgu_ref, bgu_ref, wdn_ref, bdn_ref, y_ref, wgu_bf, wdn_bf):
    j = pl.program_id(0)
    d_exp = wdn_ref.shape[1]
    used = j < nu_ref[0]
    fresh = (j == 0) | (be_ref[j] != be_ref[jnp.maximum(j - 1, 0)])
    chunk = 128

    @pl.when(used & fresh)
    def _():
        def cast_gu(c, carry):
            r0 = pl.multiple_of(c * chunk, chunk)
            wgu_bf[pl.ds(r0, chunk), :] = wgu_ref[0, pl.ds(r0, chunk), :].astype(BF16)
            return carry

        def cast_dn(c, carry):
            r0 = pl.multiple_of(c * chunk, chunk)
            wdn_bf[pl.ds(r0, chunk), :] = wdn_ref[0, pl.ds(r0, chunk), :].astype(BF16)
            return carry

        lax.fori_loop(0, wgu_bf.shape[0] // chunk, cast_gu, 0)
        lax.fori_loop(0, wdn_bf.shape[0] // chunk, cast_dn, 0)

    @pl.when(used)
    def _():
        gu = _dot(xs_ref[...].astype(BF16), wgu_bf[...]) + bgu_ref[0]
        gate = jnp.minimum(gu[:, :d_exp], SWIGLU_LIMIT)
        up = jnp.clip(gu[:, d_exp:], -SWIGLU_LIMIT, SWIGLU_LIMIT)
        act = gate * _sigmoid(gate * SWIGLU_ALPHA) * (up + 1.0)
        y_ref[...] = _dot(act.astype(BF16), wdn_bf[...]) + bdn_ref[0]

    @pl.when(jnp.logical_not(used))
    def _():
        y_ref[...] = jnp.zeros(y_ref.shape, y_ref.dtype)


def _experts(xs, block_exp, n_used, w_gu, b_gu, w_dn, b_dn):
    n_rows, d = xs.shape
    n_exp, _, gu_w = w_gu.shape
    d_exp = w_dn.shape[1]
    n_blocks = n_rows // MOE_BLOCK
    grid_spec = pltpu.PrefetchScalarGridSpec(
        num_scalar_prefetch=2, grid=(n_blocks,),
        in_specs=[pl.BlockSpec((MOE_BLOCK, d), lambda j, be, nu: (j, 0)),
                  pl.BlockSpec((1, d, gu_w), lambda j, be, nu: (be[j], 0, 0)),
                  pl.BlockSpec((1, 1, gu_w), lambda j, be, nu: (be[j], 0, 0)),
                  pl.BlockSpec((1, d_exp, d), lambda j, be, nu: (be[j], 0, 0)),
                  pl.BlockSpec((1, 1, d), lambda j, be, nu: (be[j], 0, 0))],
        out_specs=pl.BlockSpec((MOE_BLOCK, d), lambda j, be, nu: (j, 0)),
        scratch_shapes=[pltpu.VMEM((d, gu_w), BF16), pltpu.VMEM((d_exp, d), BF16)])
    return pl.pallas_call(
        _expert_kernel, grid_spec=grid_spec,
        out_shape=jax.ShapeDtypeStruct((n_rows, d), F32),
        compiler_params=_cparams(1), name="moe_experts")(
            block_exp, n_used, xs, w_gu, b_gu.reshape(n_exp, 1, gu_w), w_dn, b_dn.reshape(n_exp, 1, d))


def _combine_kernel(dest_ref, dest_next_ref, y_ref, x_ref, route_ref, g_ref, o_ref, buf, sem, *, final_norm, n_steps):
    i = pl.program_id(0)
    tt = x_ref.shape[0]
    slot = i % 2

    def row_copy(dref, into, t, k):
        return pltpu.make_async_copy(y_ref.at[pl.ds(dref[0, 0, t * TOP_K + k], 1)], buf.at[into, k, pl.ds(t, 1)],
                                     sem.at[into])

    def gather(dref, into):
        def start(t, c):
            for k in range(TOP_K):
                row_copy(dref, into, t, k).start(priority=k % 2)
            return c
        lax.fori_loop(0, tt, start, 0)

    @pl.when(i == 0)
    def _():
        gather(dest_ref, 0)

    @pl.when(i + 1 < n_steps)
    def _():
        gather(dest_next_ref, 1 - slot)

    def wait(t, c):
        for k in range(TOP_K):
            row_copy(dest_ref, slot, t, k).wait()
        return c

    lax.fori_loop(0, tt, wait, 0)
    lane = _iota((1, LANES), 1)
    rt = route_ref[...]
    moe = jnp.zeros(x_ref.shape, F32)
    for k in range(TOP_K):
        gate = jnp.sum(jnp.where(lane == ROUTE_G + k, rt, 0.0), axis=-1, keepdims=True)
        moe = moe + gate * buf[slot, k]
    out = x_ref[...] + moe
    o_ref[...] = _rms(out, g_ref[...]) if final_norm else out


def _combine(y, dest, x2, route, final_g, final_norm):
    n, d = x2.shape
    tt = MOE_COMBINE_TILE
    n_steps = n // tt
    dest3 = dest.reshape(n_steps, 1, tt * TOP_K)
    kernel = functools.partial(_combine_kernel, final_norm=final_norm, n_steps=n_steps)
    dspec = lambda f: pl.BlockSpec((1, 1, tt * TOP_K), f, memory_space=pltpu.SMEM)
    return pl.pallas_call(
        kernel, grid=(n_steps,),
        in_specs=[dspec(lambda i: (i, 0, 0)),
                  dspec(lambda i: (jnp.minimum(i + 1, n_steps - 1), 0, 0)),
                  pl.BlockSpec(memory_space=pl.ANY),
                  pl.BlockSpec((tt, d), lambda i: (i, 0)),
                  pl.BlockSpec((tt, LANES), lambda i: (i, 0)),
                  pl.BlockSpec((1, d), lambda i: (0, 0))],
        out_specs=pl.BlockSpec((tt, d), lambda i: (i, 0)),
        out_shape=jax.ShapeDtypeStruct((n, d), F32),
        scratch_shapes=[pltpu.VMEM((2, TOP_K, tt, d), F32), pltpu.SemaphoreType.DMA((2,))],
        compiler_params=_cparams(1), name="moe_combine")(dest3, dest3, y, x2, route, final_g.reshape(1, d))


def _moe(h3, x2, route, cnt, w_gu, b_gu, w_dn, b_dn, final_g, final_norm):
    n, d = h3.shape
    n_exp = w_gu.shape[0]
    e = route[:, ROUTE_E:ROUTE_E + TOP_K].astype(jnp.int32)
    rank = route[:, ROUTE_R:ROUTE_R + TOP_K].astype(jnp.int32)
    counts = cnt[0, :n_exp].astype(jnp.int32)
    padded = ((counts + MOE_BLOCK - 1) // MOE_BLOCK) * MOE_BLOCK
    pend = jnp.cumsum(padded)
    dest = (pend - padded)[e] + rank
    n_rows = n * TOP_K + n_exp * MOE_BLOCK
    n_blocks = n_rows // MOE_BLOCK
    starts = jnp.arange(n_blocks, dtype=jnp.int32) * MOE_BLOCK
    block_exp = jnp.minimum(jnp.sum((pend[None, :] <= starts[:, None]).astype(jnp.int32), axis=1), n_exp - 1)
    n_used = (pend[-1:] // MOE_BLOCK).astype(jnp.int32)
    xs = _dispatch(h3, dest, n_rows)
    y = _experts(xs, block_exp, n_used, w_gu, b_gu, w_dn, b_dn)
    return _combine(y, dest, x2, route, final_g, final_norm)


def _inproj_weights(w):
    gh = GDN_HEADS
    c = np.cumsum([0, gh * GDN_DK, gh * GDN_DK, gh * GDN_DV, gh * GDN_DV, gh, gh, NSA_HEADS * NSA_DK,
                   NSA_GROUPS * NSA_DK, NSA_GROUPS * NSA_DV, NSA_GROUPS * NSA_DK, NSA_GROUPS * NSA_DV,
                   NSA_GROUPS * NSA_DK, NSA_GROUPS * NSA_DV, NSA_HEADS * 3, D_MODEL, D_MODEL])
    n_small = 2 * gh + NSA_HEADS * 3
    small = jnp.concatenate([w[:, c[4]:c[6]], w[:, c[13]:c[14]], jnp.zeros((w.shape[0], LANES - n_small), w.dtype)],
                            axis=1)
    groups = [w[:, c[0]:c[3]],
              w[:, c[3]:c[4]],
              small,
              w[:, c[6]:c[7]],
              w[:, c[7]:c[8]],
              w[:, c[8]:c[9]],
              w[:, c[9]:c[13]],
              w[:, c[14]:c[16]]]
    return [g.astype(BF16) for g in groups]


GATE_COL0 = 2 * GDN_HEADS


def kernel(x, mem, attn_norm_g, w_in, gdn_conv_w, gdn_a_log, gdn_dt_bias, gdn_norm_g, cmp_pe_k, cmp_w1_k, cmp_w2_k, cmp_pe_v, cmp_w1_v, cmp_w2_v, w_branch_a, w_branch_b, w_mix_out, xattn_norm_g, mem_norm_g, xattn_w_q, xattn_w_kv, xattn_w_o, ffn_norm_g, router_w, router_b, w_gate_up, b_gate_up, w_down, b_down, final_norm_g):
    bsz, seq, d = x.shape
    n = bsz * seq
    depth = w_in.shape[0]
    x2d = x.reshape(n, d)
    for l in range(depth):
        gqkv, gz, small, nq, nkc, nvc, nkv, mab = _inproj(x2d, attn_norm_g[l], _inproj_weights(w_in[l]), [F32] * 8)
        b3 = lambda a: a.reshape(bsz, seq, a.shape[-1])
        o_a = _gdn(b3(gqkv), b3(gz), b3(small), gdn_conv_w[l], gdn_a_log[l], gdn_dt_bias[l], gdn_norm_g[l], bsz, seq)
        q_bf, qr_bf, kse, vsa, kw, vwa = _nsa_prep(nq, nkv, seq)
        kc = _compress(b3(nkc), cmp_pe_k[l], cmp_w1_k[l], cmp_w2_k[l])
        vc = _compress(b3(nvc), cmp_pe_v[l], cmp_w1_v[l], cmp_w2_v[l])
        oc, nind = _nsa_select(b3(q_bf), kc, vc, bsz, seq)
        o_b = _nsa_attention(b3(qr_bf), nind, oc, b3(kse), b3(vsa), b3(kw), b3(vwa), b3(small), bsz, seq, GATE_COL0)
        memkv = _mem_kv(mem, mem_norm_g[l], xattn_w_kv[l])
        x2, h3, route, cnt = _post_mixer(o_a.reshape(n, -1), o_b.reshape(n, -1), mab, x2d, w_branch_a[l],
                                         w_branch_b[l], w_mix_out[l], xattn_norm_g[l], xattn_w_q[l], memkv,
                                         xattn_w_o[l], ffn_norm_g[l], router_w[l], router_b[l], seq)
        x2d = _moe(h3, x2, route, cnt, w_gate_up[l], b_gate_up[l], w_down[l], b_down[l], final_norm_g,
                   final_norm=(l == depth - 1))
    return x2d.reshape(bsz, seq, d)
```

```python
import functools
import math

import jax
import jax.numpy as jnp
import numpy as np
from jax import lax
from jax.experimental import pallas as pl
from jax.experimental.pallas import tpu as pltpu

F32 = jnp.float32
BF16 = jnp.bfloat16
HIGHEST = lax.Precision.HIGHEST

D_MODEL = 1024
NORM_EPS = 1e-6
L2_EPS = 1e-6
ROPE_THETA = 500000.0
NEG_INF = -1e30

GDN_HEADS = 8
GDN_DK = 128
GDN_DV = 128
GDN_CONV = 4
GDN_CHUNK = 64

NSA_HEADS = 8
NSA_GROUPS = 2
NSA_REP = NSA_HEADS // NSA_GROUPS
NSA_DK = 128
NSA_DV = 128
ROT_DIM = NSA_DK // 4
CMP_LEN = 32
CMP_STRIDE = 16
SEL_LEN = 64
SEL_TOPK = 16
WINDOW = 512
Q_BLOCK = 128
SEL_FORCE = 1e4

MEM_LEN = 256
XATTN_HEADS = 4
XATTN_DH = 128

N_EXPERTS = 32
TOP_K = 4
SWIGLU_LIMIT = 7.0
SWIGLU_ALPHA = 1.702
MOE_BLOCK = 256

LANES = 128
VMEM_LIMIT = 56 * 1024 * 1024
SEL_TILE = 512


def _cparams(n_axes, vmem=VMEM_LIMIT):
    return pltpu.CompilerParams(dimension_semantics=("arbitrary",) * n_axes, vmem_limit_bytes=vmem)


def _dot(a, b, precision=None):
    return jnp.dot(a, b, preferred_element_type=F32, precision=precision)


def _dot_nt(a, b, precision=None):
    return lax.dot_general(a, b, (((1,), (1,)), ((), ())), preferred_element_type=F32, precision=precision)


def _sigmoid(x):
    return 1.0 / (1.0 + jnp.exp(-x))


def _iota(shape, dim):
    return lax.broadcasted_iota(jnp.int32, shape, dim)


def _inproj_kernel(x_ref, g_ref, *refs):
    n = len(refs) // 2
    w_refs, o_refs = refs[:n], refs[n:]
    x = x_ref[...]
    ms = jnp.mean(x * x, axis=-1, keepdims=True)
    h = (x * lax.rsqrt(ms + NORM_EPS) * g_ref[...]).astype(BF16)
    for w_ref, o_ref in zip(w_refs, o_refs):
        o_ref[...] = _dot(h, w_ref[...]).astype(o_ref.dtype)


def _inproj(x2d, gain, weights, out_dtypes, tm=256):
    n, d = x2d.shape
    in_specs = [pl.BlockSpec((tm, d), lambda i: (i, 0)), pl.BlockSpec((1, d), lambda i: (0, 0))]
    in_specs += [pl.BlockSpec(w.shape, lambda i: (0, 0), pipeline_mode=pl.Buffered(1)) for w in weights]
    out_specs = [pl.BlockSpec((tm, w.shape[1]), lambda i: (i, 0)) for w in weights]
    out_shape = [jax.ShapeDtypeStruct((n, w.shape[1]), dt) for w, dt in zip(weights, out_dtypes)]
    return pl.pallas_call(
        _inproj_kernel, grid=(n // tm,), in_specs=in_specs, out_specs=out_specs, out_shape=out_shape,
        compiler_params=_cparams(1), name="inproj")(x2d, gain.reshape(1, d), *weights)


def _rope_tables(seq):
    half = ROT_DIM // 2
    inv_freq = jnp.exp(-math.log(ROPE_THETA) * jnp.arange(half, dtype=F32) * (2.0 / ROT_DIM))
    ang = jnp.arange(seq).astype(F32)[:, None] * inv_freq[None, :]
    cos, sin = jnp.cos(ang), jnp.sin(ang)
    ones = jnp.ones((seq, LANES - ROT_DIM), F32)
    zeros = jnp.zeros((seq, LANES - ROT_DIM), F32)
    zh = jnp.zeros((seq, half), F32)
    c = jnp.concatenate([cos, cos, ones], axis=1)
    sa = jnp.concatenate([zh, sin, zeros], axis=1)
    sb = jnp.concatenate([-sin, zh, zeros], axis=1)
    return c, sa, sb


def _rope(xh, c, sa, sb):
    half = ROT_DIM // 2
    return xh * c + pltpu.roll(xh, half, 1) * sa + pltpu.roll(xh, LANES - half, 1) * sb


MASK_BIG = 2.0 ** 100
Q_PRESCALE = NSA_DK ** -0.5 * math.log2(math.e)


def _nsa_prep_kernel(nq_ref, nkv_ref, c_ref, sa_ref, sb_ref, q_ref, qr_ref, kse_ref, vsa_ref, kw_ref, vwa_ref,
                     *, rows_per_seq):
    tm = nq_ref.shape[0]
    c, sa, sb = c_ref[...], sa_ref[...], sb_ref[...]
    for h in range(NSA_HEADS):
        sl = slice(h * NSA_DK, (h + 1) * NSA_DK)
        xh = nq_ref[:, sl] * Q_PRESCALE
        q_ref[:, sl] = xh.astype(BF16)
        qr_ref[:, sl] = _rope(xh, c, sa, sb).astype(BF16)
    t0 = (pl.program_id(0) % rows_per_seq) * tm
    key_blk = jnp.right_shift(t0 + _iota((tm, 1), 0), int(math.log2(SEL_LEN)))
    blk_neg = jnp.where(key_blk == _iota((1, LANES), 1), -MASK_BIG, 0.0).astype(BF16)
    ones = jnp.ones((tm, LANES), BF16)
    for g in range(NSA_GROUPS):
        col = lambda j: nkv_ref[:, (j * NSA_GROUPS + g) * LANES:(j * NSA_GROUPS + g + 1) * LANES]
        wide = slice(2 * g * LANES, (2 * g + 1) * LANES)
        aux = slice((2 * g + 1) * LANES, (2 * g + 2) * LANES)
        kse_ref[:, wide] = _rope(col(0), c, sa, sb).astype(BF16)
        kse_ref[:, aux] = blk_neg
        vsa_ref[:, wide] = col(1).astype(BF16)
        vsa_ref[:, aux] = ones
        kw_ref[:, g * LANES:(g + 1) * LANES] = _rope(col(2), c, sa, sb).astype(BF16)
        vwa_ref[:, wide] = col(3).astype(BF16)
        vwa_ref[:, aux] = ones


def _nsa_prep(nq, nkv, seq, tm=512):
    n = nq.shape[0]
    assert seq // SEL_LEN <= LANES
    c, sa, sb = _rope_tables(seq)
    nt = seq // tm
    g = NSA_GROUPS
    row = lambda w: pl.BlockSpec((tm, w), lambda i: (i, 0))
    tab = pl.BlockSpec((tm, LANES), lambda i: (i % nt, 0))
    out_w = [nq.shape[1], nq.shape[1], 2 * g * LANES, 2 * g * LANES, g * LANES, 2 * g * LANES]
    return pl.pallas_call(
        functools.partial(_nsa_prep_kernel, rows_per_seq=nt), grid=(n // tm,),
        in_specs=[row(nq.shape[1]), row(nkv.shape[1]), tab, tab, tab],
        out_specs=[row(w) for w in out_w],
        out_shape=[jax.ShapeDtypeStruct((n, w), BF16) for w in out_w],
        compiler_params=_cparams(1), name="nsa_prep")(nq, nkv, c, sa, sb)


def _compress_kernel(x_ref, pea_ref, peb_ref, w1a_ref, w1b_ref, w2_ref, o_ref):
    x = x_ref[0]
    y0 = _dot((x + pea_ref[...]).astype(BF16), w1a_ref[...])
    y1 = _dot((x + peb_ref[...]).astype(BF16), w1b_ref[...])
    n = y1.shape[0]
    pre = y0 + pltpu.roll(y1, n - 1, 0)
    act = pre * _sigmoid(pre)
    o_ref[0] = _dot(act.astype(BF16), w2_ref[...]).astype(o_ref.dtype)


def _compress(x, pe, w1, w2):
    bsz, seq, gd = x.shape
    g = NSA_GROUPS
    d = gd // g
    half = CMP_LEN // 2
    assert CMP_STRIDE == half
    xr = x.reshape(bsz, seq // half, half * gd)
    eye = jnp.eye(g, dtype=F32)

    def expand_w1(w):
        return jnp.einsum("ldk,gh->lgdhk", w.reshape(half, d, d), eye).reshape(half * gd, gd).astype(BF16)

    w1a, w1b = expand_w1(w1[: half * d]), expand_w1(w1[half * d:])
    w2b = jnp.einsum("dk,gh->gdhk", w2, eye).reshape(gd, gd).astype(BF16)
    pea = jnp.broadcast_to(pe[:half, None, :], (half, g, d)).reshape(1, half * gd)
    peb = jnp.broadcast_to(pe[half:, None, :], (half, g, d)).reshape(1, half * gd)
    nrow = seq // half
    full = lambda a: pl.BlockSpec(a.shape, lambda b: (0,) * a.ndim)
    return pl.pallas_call(
        _compress_kernel, grid=(bsz,),
        in_specs=[pl.BlockSpec((1, nrow, half * gd), lambda b: (b, 0, 0)),
                  full(pea), full(peb), full(w1a), full(w1b), full(w2b)],
        out_specs=pl.BlockSpec((1, nrow, gd), lambda b: (b, 0, 0)),
        out_shape=jax.ShapeDtypeStruct((bsz, nrow, gd), BF16),
        compiler_params=_cparams(1), name="nsa_compress")(xr, pea, peb, w1a, w1b, w2b)


def _sel_wmap(seq):
    n_cmp = (seq - CMP_LEN) // CMP_STRIDE + 1
    n_sel = seq // SEL_LEN
    r_s = SEL_LEN // CMP_STRIDE
    r_c = CMP_LEN // CMP_STRIDE
    w = np.zeros((seq // CMP_STRIDE, n_sel), np.float32)
    for s in range(n_sel):
        for m in range(r_s):
            for nn in range(r_c):
                c = s * r_s + m - nn
                if 0 <= c < n_cmp:
                    w[c, s] += 1.0
    return w


def _heads_to_rows(ref, r0):
    return jnp.concatenate([ref[0, pl.ds(r0, Q_BLOCK), r * NSA_DK:(r + 1) * NSA_DK] for r in range(NSA_REP)], axis=0)


def _split3(x):
    hi = x.astype(BF16)
    r1 = x - hi.astype(F32)
    mid = r1.astype(BF16)
    lo = (r1 - mid.astype(F32)).astype(BF16)
    return hi, mid, lo


SELECT_QBLOCKS = 8


def _nsa_select_kernel(q_ref, kc_ref, vc_ref, wmap_ref, oc_ref, nind_ref, imp_ref, *, seq):
    j = pl.program_id(2)
    rows = NSA_REP * Q_BLOCK
    n_sel = seq // SEL_LEN
    n_cmp_pad = seq // CMP_STRIDE
    n_q = SELECT_QBLOCKS * Q_BLOCK
    base = j * n_q
    kc = kc_ref[0]
    vc = vc_ref[0]
    cmp_end = _iota((1, n_cmp_pad), 1) * CMP_STRIDE + (CMP_LEN - 1)

    def per_block(qb_i, carry):
        r0 = pl.multiple_of(qb_i * Q_BLOCK, Q_BLOCK)
        tq1 = base + r0 + _iota((Q_BLOCK, 1), 0)
        s = _dot_nt(_heads_to_rows(q_ref, r0), kc).reshape(NSA_REP, Q_BLOCK, n_cmp_pad)
        mask = (cmp_end <= tq1)[None]
        s = jnp.where(mask, s, NEG_INF)
        p = jnp.where(mask, jnp.exp2(s - jnp.max(s, axis=-1, keepdims=True)), 0.0)
        denom = jnp.maximum(jnp.sum(p, axis=-1, keepdims=True), 1e-30)
        p_c = p / denom
        o_c = _dot(p_c.reshape(rows, n_cmp_pad).astype(BF16), vc)
        for r in range(NSA_REP):
            oc_ref[0, pl.ds(r0, Q_BLOCK), r * NSA_DV:(r + 1) * NSA_DV] = o_c[r * Q_BLOCK:(r + 1) * Q_BLOCK]
        p_sum = p_c[0]
        for r in range(1, NSA_REP):
            p_sum = p_sum + p_c[r]
        hi, mid, lo = _split3(p_sum)
        w = wmap_ref[...]
        imp_ref[pl.ds(r0, Q_BLOCK), :] = _dot(hi, w) + _dot(mid, w) + _dot(lo, w)
        return carry

    lax.fori_loop(0, SELECT_QBLOCKS, per_block, 0)

    n_pick = SEL_TOPK - 3
    tq = base + _iota((n_q, 1), 0)
    sel_ids = _iota((1, LANES), 1)
    cur = jnp.right_shift(tq, int(math.log2(SEL_LEN)))
    forced = (sel_ids == 0) | (sel_ids == cur) | (sel_ids == cur - 1)
    causal_blk = sel_ids * SEL_LEN <= tq
    cand = jnp.where(causal_blk & jnp.logical_not(forced), imp_ref[...], -jnp.inf)

    def write(picked):
        keep = forced | (picked > 0.5)
        nind_ref[0, 0] = jnp.where(causal_blk & keep, 0.0, 1.0).astype(BF16)

    score = cand
    picked = jnp.zeros((n_q, LANES), F32)
    for _ in range(n_pick):
        mx = jnp.max(score, axis=-1, keepdims=True)
        pick = (score == mx) & (mx > -jnp.inf)
        picked = jnp.where(pick, 1.0, picked)
        score = jnp.where(pick, -jnp.inf, score)
    write(picked)
    most = jnp.max(jnp.sum(picked, axis=-1, keepdims=True))

    @pl.when(most > n_pick)
    def _():
        lane = _iota((n_q, LANES), 1).astype(F32)
        score = cand
        picked = jnp.zeros((n_q, LANES), F32)
        for _ in range(n_pick):
            mx = jnp.max(score, axis=-1, keepdims=True)
            hit = (score == mx) & (mx > -jnp.inf)
            first = jnp.min(jnp.where(hit, lane, float(LANES)), axis=-1, keepdims=True)
            pick = lane == first
            picked = jnp.where(pick, 1.0, picked)
            score = jnp.where(pick, -jnp.inf, score)
        write(picked)


def _nsa_select(q, kc, vc, bsz, seq):
    g = NSA_GROUPS
    n_q = SELECT_QBLOCKS * Q_BLOCK
    wmap = jnp.asarray(_sel_wmap(seq), BF16)
    wmap = jnp.pad(wmap, ((0, 0), (0, LANES - wmap.shape[1])))
    cspec = pl.BlockSpec((1, seq // CMP_STRIDE, LANES), lambda b, gg, j: (b, 0, gg))
    kernel = functools.partial(_nsa_select_kernel, seq=seq)
    return pl.pallas_call(
        kernel, grid=(bsz, g, seq // n_q),
        in_specs=[pl.BlockSpec((1, n_q, NSA_REP * NSA_DK), lambda b, gg, j: (b, j, gg)), cspec, cspec,
                  pl.BlockSpec(wmap.shape, lambda b, gg, j: (0, 0))],
        out_specs=[pl.BlockSpec((1, n_q, NSA_REP * NSA_DV), lambda b, gg, j: (b, j, gg)),
                   pl.BlockSpec((1, 1, n_q, LANES), lambda b, gg, j: (b, gg, j, 0))],
        out_shape=[jax.ShapeDtypeStruct((bsz, seq, NSA_HEADS * NSA_DV), F32),
                   jax.ShapeDtypeStruct((bsz, g, seq, LANES), BF16)],
        scratch_shapes=[pltpu.VMEM((n_q, LANES), F32)],
        compiler_params=_cparams(3), name="nsa_select")(q, kc, vc, wmap)


def _nsa_attn_kernel(qr_ref, nind_ref, oc_ref, kse_ref, vsa_ref, kw_ref, vwa_ref, gate_ref, o_ref, m_ref, acc_ref,
                     sa_ref, sb_ref, *, gate_col0):
    g = pl.program_id(1)
    i = pl.program_id(2)
    q0 = i * Q_BLOCK
    rows = NSA_REP * Q_BLOCK
    tq1 = q0 + _iota((Q_BLOCK, 1), 0)

    qrb = _heads_to_rows(qr_ref, 0)
    lhs = jnp.concatenate([qrb, jnp.concatenate([nind_ref[0, 0]] * NSA_REP, axis=0)], axis=1)

    m_ref[...] = jnp.full(m_ref.shape, -MASK_BIG, F32)
    acc_ref[...] = jnp.zeros(acc_ref.shape, F32)

    n_tiles = q0 // SEL_TILE + 1

    def tile_start(t):
        return pl.multiple_of(jnp.minimum(t, n_tiles - 1) * SEL_TILE, SEL_TILE)

    def scores(t):
        return _dot_nt(lhs, kse_ref[0, pl.ds(tile_start(t), SEL_TILE), :])

    def tile_terms(s, t):
        k0 = tile_start(t)
        ok = (k0 + _iota((1, SEL_TILE), 1) <= tq1) & (t < n_tiles)
        bias = jnp.where(ok, 0.0, -MASK_BIG)
        s = (s.reshape(NSA_REP, Q_BLOCK, SEL_TILE) + bias[None]).reshape(rows, SEL_TILE)
        m_t = jnp.max(s, axis=-1, keepdims=True)
        p = jnp.exp2(s - m_t).astype(BF16)
        return m_t, _dot(p, vsa_ref[0, pl.ds(k0, SEL_TILE), :])

    def merge(terms):
        m_new = m_ref[...]
        for m_t, _ in terms:
            m_new = jnp.maximum(m_new, m_t)
        acc = jnp.exp2(m_ref[...] - m_new) * acc_ref[...]
        for m_t, pv in terms:
            acc = acc + jnp.exp2(m_t - m_new) * pv
        acc_ref[...] = acc
        m_ref[...] = m_new

    sa_ref[...] = scores(0)

    def tile_pair(p, carry):
        t0 = 2 * p
        sb_ref[...] = scores(t0 + 1)
        first = tile_terms(sa_ref[...], t0)
        sa_ref[...] = scores(t0 + 2)
        merge([first, tile_terms(sb_ref[...], t0 + 1)])
        return carry

    lax.fori_loop(0, (n_tiles + 1) // 2, tile_pair, 0)
    o_s = acc_ref[:, :NSA_DV] / acc_ref[:, NSA_DV:]

    wlen = WINDOW + Q_BLOCK
    w0 = pl.multiple_of(jnp.maximum(q0 - WINDOW, 0), Q_BLOCK)
    s_w = _dot_nt(qrb, kw_ref[0, pl.ds(w0, wlen), :]).reshape(NSA_REP, Q_BLOCK, wlen)
    dlt = tq1 - (w0 + _iota((1, wlen), 1))
    band = jnp.where((dlt >= 0) & (dlt < WINDOW), 0.0, -MASK_BIG)
    s_w = (s_w + band[None]).reshape(rows, wlen)
    p_w = jnp.exp2(s_w - jnp.max(s_w, axis=-1, keepdims=True)).astype(BF16)
    pv_w = _dot(p_w, vwa_ref[0, pl.ds(w0, wlen), :])
    o_w = pv_w[:, :NSA_DV] / pv_w[:, NSA_DV:]

    gts = _sigmoid(gate_ref[0])
    for r in range(NSA_REP):
        rs = slice(r * Q_BLOCK, (r + 1) * Q_BLOCK)
        cols = []
        for j in range(3):
            col = gate_col0 + (g * NSA_REP + r) * 3 + j
            onehot = _iota((1, LANES), 1) == col
            cols.append(jnp.sum(jnp.where(onehot, gts, 0.0), axis=-1, keepdims=True))
        o_c = oc_ref[0, :, r * NSA_DV:(r + 1) * NSA_DV]
        out = cols[0] * o_c + cols[1] * o_s[rs] + cols[2] * o_w[rs]
        o_ref[0, :, r * NSA_DV:(r + 1) * NSA_DV] = out.astype(o_ref.dtype)


def _nsa_attention(qr, nind, oc, kse, vsa, kw, vwa, small, bsz, seq, gate_col0):
    g = NSA_GROUPS
    rows = NSA_REP * Q_BLOCK
    qspec = pl.BlockSpec((1, Q_BLOCK, NSA_REP * NSA_DK), lambda b, gg, i: (b, i, gg))
    wide = pl.BlockSpec((1, seq, 2 * LANES), lambda b, gg, i: (b, 0, gg))
    kernel = functools.partial(_nsa_attn_kernel, gate_col0=gate_col0)
    return pl.pallas_call(
        kernel, grid=(bsz, g, seq // Q_BLOCK),
        in_specs=[qspec,
                  pl.BlockSpec((1, 1, Q_BLOCK, LANES), lambda b, gg, i: (b, gg, i, 0)),
                  qspec, wide, wide,
                  pl.BlockSpec((1, seq, LANES), lambda b, gg, i: (b, 0, gg)),
                  wide,
                  pl.BlockSpec((1, Q_BLOCK, LANES), lambda b, gg, i: (b, i, 0))],
        out_specs=qspec,
        out_shape=jax.ShapeDtypeStruct((bsz, seq, NSA_HEADS * NSA_DV), BF16),
        scratch_shapes=[pltpu.VMEM((rows, 1), F32), pltpu.VMEM((rows, 2 * NSA_DV), F32),
                        pltpu.VMEM((rows, SEL_TILE), F32), pltpu.VMEM((rows, SEL_TILE), F32)],
        compiler_params=_cparams(3), name="nsa_attn")(qr, nind, oc, kse, vsa, kw, vwa, small)


GDN_UNIT = 2 * GDN_CHUNK
GDN_STEP_UNITS = 2
CONV_HALO = 8


def _lcat(a, b):
    return jnp.concatenate([a, b], axis=1)


def _bdiag(a, b):
    za = jnp.zeros((a.shape[0], b.shape[1]), a.dtype)
    zb = jnp.zeros((b.shape[0], a.shape[1]), b.dtype)
    return jnp.concatenate([_lcat(a, za), _lcat(zb, b)], axis=0)


def _hilo(x):
    hi = x.astype(BF16)
    return hi, (x - hi.astype(F32)).astype(BF16)


def _rhs_of(hl):
    m = hl[0].shape[1] // 2
    return _bdiag(hl[0][:, :m], hl[0][:, m:]), _bdiag(hl[1][:, :m], hl[1][:, m:])


def _pair_rhs(x):
    return _rhs_of(_hilo(x))


def _mm3(lhs, rhs):
    return _dot(lhs[0], rhs[0]) + _dot(lhs[0], rhs[1]) + _dot(lhs[1], rhs[0])


def _pair_lower_inverse(mats, eye, b16, b32):
    eye_b = eye.astype(BF16)
    split = lambda xs: [_hilo(x) for x in xs]
    sq = lambda hls: [_mm3(h, _rhs_of(h)) for h in hls]
    plus = lambda h: (h[0] + eye_b, h[1])
    minus = lambda h: (eye_b - h[0], -h[1])
    d_hl = split([jnp.where(b16, a, 0.0) for a in mats])
    d2_hl = split(sq(d_hl))
    d4_hl = split(sq(d2_hl))
    d8_hl = split(sq(d4_hl))
    t = [_mm3(minus(a), _rhs_of(plus(b))) for a, b in zip(d_hl, d2_hl)]
    t = [_mm3(a, _rhs_of(plus(b))) for a, b in zip(split(t), d4_hl)]
    t = [_mm3(a, _rhs_of(plus(b))) for a, b in zip(split(t), d8_hl)]
    for pick in (lambda a: jnp.where(b32 & jnp.logical_not(b16), a, 0.0), lambda a: jnp.where(b32, 0.0, a)):
        t_hl = split(t)
        inner = [_mm3(_hilo(pick(a)), _rhs_of(h)) for a, h in zip(mats, t_hl)]
        t = [x - _mm3(h, _pair_rhs(y)) for x, h, y in zip(t, t_hl, inner)]
    return t


def _gdn_kernel(x_ref, halo_ref, z_ref, small_ref, convw_ref, alog_ref, dtb_ref, ng_ref, o_ref, s_ref):
    i = pl.program_id(1)
    u_len = GDN_UNIT
    c_len = GDN_CHUNK
    dk, dv = GDN_DK, GDN_DV
    hq = GDN_HEADS * dk

    @pl.when(i == 0)
    def _():
        s_ref[...] = jnp.zeros(s_ref.shape, F32)

    row = _iota((u_len, 2 * u_len), 0)
    col = _iota((u_len, 2 * u_len), 1) & (u_len - 1)
    same = jnp.right_shift(row, 6) == jnp.right_shift(col, 6)
    incl = same & (row >= col)
    strict = same & (row > col)
    b16 = jnp.right_shift(row, 4) == jnp.right_shift(col, 4)
    b32 = jnp.right_shift(row, 5) == jnp.right_shift(col, 5)
    eye = (row == col).astype(F32)
    eye1 = eye[:, :u_len] > 0.5
    rcol = _iota((u_len, 1), 0)
    lane = _iota((1, LANES), 1)
    first_rows = rcol < c_len
    first_cols = (_iota((1, 2 * u_len), 1) & (u_len - 1)) < c_len
    head0 = _iota((1, 2 * dv), 1) < dv
    halo_on = (i > 0).astype(F32)

    cum = incl[:, :u_len].astype(BF16)
    beta_all, gc_all = [], []
    for un in range(GDN_STEP_UNITS):
        sm = small_ref[0, un * u_len:(un + 1) * u_len, :]
        beta_all.append(_sigmoid(sm))
        xs = sm + dtb_ref[...]
        softplus = jnp.maximum(xs, 0.0) + jnp.log1p(jnp.exp(-jnp.abs(xs)))
        ld_hi, ld_mid, ld_lo = _split3(-jnp.exp(alog_ref[...]) * softplus)
        gc_all.append(_dot(cum, ld_hi) + _dot(cum, ld_mid) + _dot(cum, ld_lo))

    def pick_col(a, c):
        return jnp.sum(jnp.where(lane == c, a, 0.0), axis=-1, keepdims=True)

    def conv_silu(c0, un):
        r0 = un * u_len
        before = (halo_ref[0, :, c0:c0 + LANES] * halo_on if un == 0 else x_ref[0, r0 - CONV_HALO:r0, c0:c0 + LANES])
        xf = jnp.concatenate([before, x_ref[0, r0:r0 + u_len, c0:c0 + LANES]], axis=0)
        w = convw_ref[:, c0:c0 + LANES]
        y = w[0:1] * xf[CONV_HALO - 3:CONV_HALO - 3 + u_len]
        for j in range(1, GDN_CONV):
            off = CONV_HALO - (GDN_CONV - 1) + j
            y = y + w[j:j + 1] * xf[off:off + u_len]
        return y * _sigmoid(y)

    def head_terms(h, un):
        q = conv_silu(h * dk, un)
        k = conv_silu(hq + h * dk, un)
        v = conv_silu(2 * hq + h * dv, un)
        q = q * lax.rsqrt(jnp.sum(q * q, axis=-1, keepdims=True) + L2_EPS) * (dk ** -0.5)
        k = k * lax.rsqrt(jnp.sum(k * k, axis=-1, keepdims=True) + L2_EPS)
        beta = pick_col(beta_all[un], h)
        gcol = pick_col(gc_all[un], GDN_HEADS + h)
        grow = jnp.sum(jnp.where(eye1, gcol, 0.0), axis=0, keepdims=True)
        gl0 = jnp.sum(jnp.where(rcol == c_len - 1, gcol, 0.0), axis=0, keepdims=True)
        gl1 = jnp.sum(jnp.where(rcol == u_len - 1, gcol, 0.0), axis=0, keepdims=True)
        e_g = jnp.exp(gcol)
        kb = k * beta
        return dict(q=q, kb=kb, k_t=k.T, vb_kw=_lcat(v * beta, kb * e_g), qd=q * e_g,
                    diff=gcol - grow, kd_scale=jnp.exp(jnp.where(first_cols[:, :u_len], gl0, gl1) - grow),
                    g0=jnp.exp(gl0), g1=jnp.exp(gl1))

    n_pairs = GDN_HEADS // 2
    prs = range(n_pairs)

    def unit_local(un):
        pa = [head_terms(2 * p, un) for p in prs]
        pb = [head_terms(2 * p + 1, un) for p in prs]
        decay = [jnp.where(incl, jnp.exp(jnp.where(incl, _lcat(a["diff"], b["diff"]), 0.0)), 0.0)
                 for a, b in zip(pa, pb)]
        kt_rhs = [_pair_rhs(_lcat(a["k_t"], b["k_t"])) for a, b in zip(pa, pb)]
        gram = [_mm3(_hilo(_lcat(a["kb"], b["kb"])), r) for a, b, r in zip(pa, pb, kt_rhs)]
        t_mat = _pair_lower_inverse([jnp.where(strict, g_ * d_, 0.0) for g_, d_ in zip(gram, decay)], eye, b16, b32)
        vk = [(_hilo(a["vb_kw"]), _hilo(b["vb_kw"])) for a, b in zip(pa, pb)]
        uw = [_mm3(_hilo(t), (_bdiag(va[0], vb[0]), _bdiag(va[1], vb[1])))
              for t, (va, vb) in zip(t_mat, vk)]
        u = [_lcat(x[:, :dv], x[:, dv + dk:2 * dv + dk]) for x in uw]
        w = [_lcat(x[:, dv:dv + dk], x[:, 2 * dv + dk:]).astype(BF16) for x in uw]
        qk = [jnp.where(incl, _dot(_lcat(a["q"], b["q"]).astype(BF16), r[0]) * d_, 0.0).astype(BF16)
              for a, b, r, d_ in zip(pa, pb, kt_rhs, decay)]
        w_qd = [jnp.concatenate([w_, _lcat(a["qd"], b["qd"]).astype(BF16)], axis=0)
                for w_, a, b in zip(w, pa, pb)]
        kd = [_lcat(a["k_t"] * a["kd_scale"], b["k_t"] * b["kd_scale"]) for a, b in zip(pa, pb)]
        qk_kd = [[jnp.concatenate([qk_, jnp.where(first_cols, x, 0.0).astype(BF16)], axis=0) for qk_, x in zip(qk, kd)],
                 [jnp.concatenate([qk_, jnp.where(first_cols, 0.0, x).astype(BF16)], axis=0) for qk_, x in zip(qk, kd)]]
        gains = [[jnp.where(head0, a[key], b[key]) for a, b in zip(pa, pb)] for key in ("g0", "g1")]
        return u, w_qd, qk_kd, gains

    local = [unit_local(un) for un in range(GDN_STEP_UNITS)]
    s = [s_ref[p] for p in prs]
    for un in range(GDN_STEP_UNITS):
        u, w_qd, qk_kd, gains = local[un]
        outs = []
        for c in range(2):
            sb = [x.astype(BF16) for x in s]
            r1 = [_dot(w_qd[p], _bdiag(sb[p][:, :dv], sb[p][:, dv:])) for p in prs]
            vn = [(u[p] - r1[p][:u_len]).astype(BF16) for p in prs]
            r2 = [_dot(qk_kd[c][p], _bdiag(vn[p][:, :dv], vn[p][:, dv:])) for p in prs]
            outs.append([r1[p][u_len:] + r2[p][:u_len] for p in prs])
            s = [s[p] * gains[c][p] + r2[p][u_len:] for p in prs]
        for p in prs:
            o_pair = jnp.where(first_rows, outs[0][p], outs[1][p])
            for j in range(2):
                h = 2 * p + j
                rows = slice(un * u_len, (un + 1) * u_len)
                o = o_pair[:, j * dv:(j + 1) * dv]
                o = o * lax.rsqrt(jnp.mean(o * o, axis=-1, keepdims=True) + NORM_EPS) * ng_ref[...]
                z = z_ref[0, rows, h * dv:(h + 1) * dv]
                o_ref[0, rows, h * dv:(h + 1) * dv] = (o * (z * _sigmoid(z))).astype(o_ref.dtype)
    for p in prs:
        s_ref[p] = s[p]


def _gdn(gqkv, gz, small, conv_w, a_log, dt_bias, norm_g, bsz, seq):
    u_len = GDN_UNIT * GDN_STEP_UNITS
    c_all = gqkv.shape[-1]
    hv = GDN_HEADS * GDN_DV
    pad = jnp.zeros((GDN_HEADS,), F32)
    tail = jnp.zeros((LANES - 2 * GDN_HEADS,), F32)
    alog_row = jnp.concatenate([pad, a_log.astype(F32), tail]).reshape(1, LANES)
    dtb_row = jnp.concatenate([pad, dt_bias.astype(F32), tail]).reshape(1, LANES)
    per_halo = u_len // CONV_HALO
    return pl.pallas_call(
        _gdn_kernel, grid=(bsz, seq // u_len),
        in_specs=[pl.BlockSpec((1, u_len, c_all), lambda b, i: (b, i, 0)),
                  pl.BlockSpec((1, CONV_HALO, c_all), lambda b, i: (b, jnp.maximum(i * per_halo - 1, 0), 0)),
                  pl.BlockSpec((1, u_len, hv), lambda b, i: (b, i, 0)),
                  pl.BlockSpec((1, u_len, LANES), lambda b, i: (b, i, 0)),
                  pl.BlockSpec(conv_w.shape, lambda b, i: (0, 0)),
                  pl.BlockSpec((1, LANES), lambda b, i: (0, 0)),
                  pl.BlockSpec((1, LANES), lambda b, i: (0, 0)),
                  pl.BlockSpec((1, GDN_DV), lambda b, i: (0, 0))],
        out_specs=pl.BlockSpec((1, u_len, hv), lambda b, i: (b, i, 0)),
        out_shape=jax.ShapeDtypeStruct((bsz, seq, hv), BF16),
        scratch_shapes=[pltpu.VMEM((GDN_HEADS // 2, GDN_DK, 2 * GDN_DV), F32)],
        compiler_params=_cparams(2), name="gdn")(gqkv, gqkv, gz, small, conv_w, alog_row, dtb_row,
                                                  norm_g.reshape(1, GDN_DV))


def _rms(x, gain):
    return x * lax.rsqrt(jnp.mean(x * x, axis=-1, keepdims=True) + NORM_EPS) * gain


def _mem_kv_kernel(m_ref, g_ref, w_ref, o_ref):
    h = _rms(m_ref[0], g_ref[...]).astype(BF16)
    o_ref[0] = _dot(h, w_ref[...]).astype(o_ref.dtype)


def _mem_kv(mem, gain, w_kv):
    bsz, m_len, d = mem.shape
    w = w_kv.astype(BF16)
    return pl.pallas_call(
        _mem_kv_kernel, grid=(bsz,),
        in_specs=[pl.BlockSpec((1, m_len, d), lambda b: (b, 0, 0)), pl.BlockSpec((1, d), lambda b: (0, 0)),
                  pl.BlockSpec(w.shape, lambda b: (0, 0))],
        out_specs=pl.BlockSpec((1, m_len, w.shape[1]), lambda b: (b, 0, 0)),
        out_shape=jax.ShapeDtypeStruct((bsz, m_len, w.shape[1]), BF16),
        compiler_params=_cparams(1), name="mem_kv")(mem, gain.reshape(1, d), w)


ROUTE_E, ROUTE_G, ROUTE_R = 0, TOP_K, 2 * TOP_K


def _store_token_tiles(ref, row0, x):
    m, d = x.shape
    g = d // LANES
    for j in range(g):
        ref[pl.ds(row0 * g + j, m, stride=g), :] = x[:, j * LANES:(j + 1) * LANES]


def _load_token_tiles(ref, idx, m, g):
    return jnp.concatenate([ref[idx + (pl.ds(j, m, stride=g), slice(None))] for j in range(g)], axis=1)


def _post_mixer_kernel(oa_ref, ob_ref, mab_ref, x_ref, wa_ref, wb_ref, wmix_ref, gx_ref, wq_ref, km_ref, vm_ref,
                       wo_ref, gf_ref, rw_ref, rb_ref, x2_ref, h3_ref, route_ref, cnt_ref, run_ref):
    i = pl.program_id(0)
    tm, d = x_ref.shape
    ts = POST_SUB
    subs = [slice(j * ts, (j + 1) * ts) for j in range(tm // ts)]

    @pl.when(i == 0)
    def _():
        run_ref[...] = jnp.zeros(run_ref.shape, F32)

    ya = [_dot(oa_ref[r, :], wa_ref[...]) for r in subs]
    yb = [_dot(ob_ref[r, :], wb_ref[...]) for r in subs]
    mixed = [(_sigmoid(mab_ref[r, :d]) * a + _sigmoid(mab_ref[r, d:]) * b).astype(BF16) for r, a, b in zip(subs, ya, yb)]
    x1 = [x_ref[r, :] + _dot(m, wmix_ref[...]) for r, m in zip(subs, mixed)]

    q = [_dot(_rms(x, gx_ref[...]).astype(BF16), wq_ref[...]).astype(BF16) for x in x1]
    heads = []
    for hd in range(XATTN_HEADS):
        sl = slice(hd * XATTN_DH, (hd + 1) * XATTN_DH)
        s = [_dot_nt(qq[:, sl], km_ref[0, :, sl]) * (XATTN_DH ** -0.5) for qq in q]
        p = [jnp.exp(x - jnp.max(x, axis=-1, keepdims=True)) for x in s]
        p = [(x / jnp.sum(x, axis=-1, keepdims=True)).astype(BF16) for x in p]
        heads.append([_dot(x, vm_ref[0, :, sl]) for x in p])
    o = [jnp.concatenate([heads[hd][j] for hd in range(XATTN_HEADS)], axis=-1).astype(BF16) for j in range(len(subs))]
    x2 = [x + _dot(oo, wo_ref[...]) for x, oo in zip(x1, o)]

    h3 = [_rms(x, gf_ref[...]) for x in x2]
    logits = []
    for h in h3:
        acc = rb_ref[...]
        for term in _split3(h):
            both = _dot(term, rw_ref[...])
            acc = acc + (both[:, :LANES] + both[:, LANES:])
        logits.append(acc)
    for r, x, h in zip(subs, x2, h3):
        x2_ref[r, :] = x
        _store_token_tiles(h3_ref, r.start, h)

    lane = _iota((ts, LANES), 1).astype(F32)
    work = logits
    vals, idxs = [], []
    for _ in range(TOP_K):
        mx = [jnp.max(w, axis=-1, keepdims=True) for w in work]
        first = [jnp.min(jnp.where(w == m, lane, float(LANES)), axis=-1, keepdims=True) for w, m in zip(work, mx)]
        vals.append(mx)
        idxs.append(first)
        work = [jnp.where(lane == f, -jnp.inf, w) for w, f in zip(work, first)]
    earlier = (_iota((ts, ts), 0) > _iota((ts, ts), 1)).astype(BF16)
    run = run_ref[...]
    for j, r in enumerate(subs):
        exps = [jnp.exp(vals[k][j] - vals[0][j]) for k in range(TOP_K)]
        den = exps[0]
        for e in exps[1:]:
            den = den + e
        onehot = jnp.zeros((ts, LANES), F32)
        for k in range(TOP_K):
            onehot = onehot + (lane == idxs[k][j]).astype(F32)
        before = run + _dot(earlier, onehot.astype(BF16))
        route = jnp.zeros((ts, LANES), F32)
        for k in range(TOP_K):
            rank = jnp.sum(jnp.where(lane == idxs[k][j], before, 0.0), axis=-1, keepdims=True)
            route = jnp.where(lane == float(ROUTE_E + k), idxs[k][j], route)
            route = jnp.where(lane == float(ROUTE_G + k), exps[k] / den, route)
            route = jnp.where(lane == float(ROUTE_R + k), rank, route)
        route_ref[r, :] = route
        run = run + jnp.sum(onehot, axis=0, keepdims=True)
    run_ref[...] = run
    cnt_ref[...] = run


POST_SUB = 256


def _post_mixer(o_a, o_b, mab, x2d, w_a, w_b, w_mix, gx, w_q, memkv, w_o, gf, router_w, router_b, seq, tm=512):
    n, d = x2d.shape
    xd = w_q.shape[1]
    m_len = memkv.shape[1]
    per_b = seq // tm
    n_exp = router_w.shape[1]
    rw32 = jnp.concatenate([router_w.astype(F32), jnp.zeros((d, LANES - n_exp), F32)], axis=1)
    rw_hi = rw32.astype(BF16)
    rw = jnp.concatenate([rw_hi, (rw32 - rw_hi.astype(F32)).astype(BF16)], axis=1)
    rb = jnp.concatenate([router_b.astype(F32), jnp.full((LANES - n_exp,), NEG_INF, F32)]).reshape(1, LANES)
    row = lambda w: pl.BlockSpec((tm, w), lambda i: (i, 0))
    full = lambda a: pl.BlockSpec(a.shape, lambda i: (0,) * a.ndim)
    wa, wb, wm, wq, wo = (w.astype(BF16) for w in (w_a, w_b, w_mix, w_q, w_o))
    gx2, gf2 = gx.reshape(1, d), gf.reshape(1, d)
    return pl.pallas_call(
        _post_mixer_kernel, grid=(n // tm,),
        in_specs=[row(d), row(d), row(2 * d), row(d), full(wa), full(wb), full(wm), full(gx2), full(wq),
                  pl.BlockSpec((1, m_len, xd), lambda i: (i // per_b, 0, 0)),
                  pl.BlockSpec((1, m_len, xd), lambda i: (i // per_b, 0, 1)),
                  full(wo), full(gf2), full(rw), full(rb)],
        out_specs=[row(d), pl.BlockSpec((tm * (d // LANES), LANES), lambda i: (i, 0)), row(LANES),
                   pl.BlockSpec((1, LANES), lambda i: (0, 0))],
        out_shape=[jax.ShapeDtypeStruct((n, d), F32), jax.ShapeDtypeStruct((n * (d // LANES), LANES), F32),
                   jax.ShapeDtypeStruct((n, LANES), F32), jax.ShapeDtypeStruct((1, LANES), F32)],
        scratch_shapes=[pltpu.VMEM((1, LANES), F32)],
        compiler_params=_cparams(1), name="post_mixer")(o_a, o_b, mab, x2d, wa, wb, wm, gx2, wq, memkv, memkv, wo,
                                                        gf2, rw, rb)


MOE_DISPATCH_TILE = 512
MOE_COMBINE_TILE = 256
DMA_LOOP_UNROLL = 4


TOKEN_TILE = D_MODEL // LANES


def _token_rows(t):
    return pl.ds(pl.multiple_of(t * TOKEN_TILE, TOKEN_TILE), TOKEN_TILE)


def _dispatch_kernel(dest_ref, h_ref, xs_in_ref, xs_ref, sem):
    del xs_in_ref
    tt = h_ref.shape[0] // TOKEN_TILE

    def row_copy(t, dst_row):
        return pltpu.make_async_copy(h_ref.at[_token_rows(t)], xs_ref.at[_token_rows(dst_row)], sem)

    def start(t, c):
        for k in range(TOP_K):
            row_copy(t, dest_ref[0, 0, t * TOP_K + k]).start(priority=k % 2)
        return c

    def wait(t, c):
        for k in range(TOP_K):
            row_copy(t, dest_ref[0, 0, t * TOP_K + k]).wait()
        return c

    lax.fori_loop(0, tt, start, 0, unroll=DMA_LOOP_UNROLL)
    lax.fori_loop(0, tt, wait, 0, unroll=DMA_LOOP_UNROLL)


def _dispatch(h3, dest, n_rows):
    n = h3.shape[0] // TOKEN_TILE
    tt = MOE_DISPATCH_TILE
    dest3 = dest.reshape(n // tt, 1, tt * TOP_K)
    xs0 = jnp.zeros((n_rows * TOKEN_TILE, LANES), h3.dtype)
    return pl.pallas_call(
        _dispatch_kernel, grid=(n // tt,),
        in_specs=[pl.BlockSpec((1, 1, tt * TOP_K), lambda i: (i, 0, 0), memory_space=pltpu.SMEM),
                  pl.BlockSpec((tt * TOKEN_TILE, LANES), lambda i: (i, 0)),
                  pl.BlockSpec(memory_space=pl.ANY)],
        out_specs=pl.BlockSpec(memory_space=pl.ANY),
        out_shape=jax.ShapeDtypeStruct((n_rows * TOKEN_TILE, LANES), h3.dtype),
        scratch_shapes=[pltpu.SemaphoreType.DMA(())],
        input_output_aliases={2: 0},
        compiler_params=_cparams(1), name="moe_dispatch")(dest3, h3, xs0)


def _expert_kernel(be_ref, nu_ref, xs_ref, wgu_ref, bgu_ref, wdn_ref, bdn_ref, y_ref, wgu_bf, wdn_bf):
    j = pl.program_id(0)
    d_exp = wdn_ref.shape[1]
    used = j < nu_ref[0]
    fresh = (j == 0) | (be_ref[j] != be_ref[jnp.maximum(j - 1, 0)])
    chunk = 128

    @pl.when(used & fresh)
    def _():
        def cast_gu(c, carry):
            r0 = pl.multiple_of(c * chunk, chunk)
            wgu_bf[pl.ds(r0, chunk), :] = wgu_ref[0, pl.ds(r0, chunk), :].astype(BF16)
            return carry

        def cast_dn(c, carry):
            r0 = pl.multiple_of(c * chunk, chunk)
            wdn_bf[pl.ds(r0, chunk), :] = wdn_ref[0, pl.ds(r0, chunk), :].astype(BF16)
            return carry

        lax.fori_loop(0, wgu_bf.shape[0] // chunk, cast_gu, 0)
        lax.fori_loop(0, wdn_bf.shape[0] // chunk, cast_dn, 0)

    @pl.when(used)
    def _():
        x = _load_token_tiles(xs_ref, (), MOE_BLOCK, TOKEN_TILE).astype(BF16)
        gu = _dot(x, wgu_bf[...]) + bgu_ref[0]
        gate = jnp.minimum(gu[:, :d_exp], SWIGLU_LIMIT)
        up = jnp.clip(gu[:, d_exp:], -SWIGLU_LIMIT, SWIGLU_LIMIT)
        act = gate * _sigmoid(gate * SWIGLU_ALPHA) * (up + 1.0)
        _store_token_tiles(y_ref, 0, _dot(act.astype(BF16), wdn_bf[...]) + bdn_ref[0])

    @pl.when(jnp.logical_not(used))
    def _():
        y_ref[...] = jnp.zeros(y_ref.shape, y_ref.dtype)


def _experts(xs, block_exp, n_used, w_gu, b_gu, w_dn, b_dn):
    n_exp, d, gu_w = w_gu.shape
    n_rows = xs.shape[0] // TOKEN_TILE
    d_exp = w_dn.shape[1]
    n_blocks = n_rows // MOE_BLOCK
    tiles = pl.BlockSpec((MOE_BLOCK * TOKEN_TILE, LANES), lambda j, be, nu: (j, 0))
    grid_spec = pltpu.PrefetchScalarGridSpec(
        num_scalar_prefetch=2, grid=(n_blocks,),
        in_specs=[tiles,
                  pl.BlockSpec((1, d, gu_w), lambda j, be, nu: (be[j], 0, 0)),
                  pl.BlockSpec((1, 1, gu_w), lambda j, be, nu: (be[j], 0, 0)),
                  pl.BlockSpec((1, d_exp, d), lambda j, be, nu: (be[j], 0, 0)),
                  pl.BlockSpec((1, 1, d), lambda j, be, nu: (be[j], 0, 0))],
        out_specs=tiles,
        scratch_shapes=[pltpu.VMEM((d, gu_w), BF16), pltpu.VMEM((d_exp, d), BF16)])
    return pl.pallas_call(
        _expert_kernel, grid_spec=grid_spec,
        out_shape=jax.ShapeDtypeStruct(xs.shape, F32),
        compiler_params=_cparams(1), name="moe_experts")(
            block_exp, n_used, xs, w_gu, b_gu.reshape(n_exp, 1, gu_w), w_dn, b_dn.reshape(n_exp, 1, d))


def _combine_kernel(dest_ref, dest_next_ref, y_ref, x_ref, route_ref, g_ref, o_ref, buf, sem, *, final_norm, n_steps):
    i = pl.program_id(0)
    tt = x_ref.shape[0]
    slot = i % 2

    def row_copy(dref, into, t, k):
        return pltpu.make_async_copy(y_ref.at[_token_rows(dref[0, 0, t * TOP_K + k])], buf.at[into, k, _token_rows(t)],
                                     sem.at[into])

    def gather(dref, into):
        def start(t, c):
            for k in range(TOP_K):
                row_copy(dref, into, t, k).start(priority=k % 2)
            return c
        lax.fori_loop(0, tt, start, 0, unroll=DMA_LOOP_UNROLL)

    @pl.when(i == 0)
    def _():
        gather(dest_ref, 0)

    @pl.when(i + 1 < n_steps)
    def _():
        gather(dest_next_ref, 1 - slot)

    def wait(t, c):
        for k in range(TOP_K):
            row_copy(dest_ref, slot, t, k).wait()
        return c

    lax.fori_loop(0, tt, wait, 0, unroll=DMA_LOOP_UNROLL)
    lane = _iota((1, LANES), 1)
    rt = route_ref[...]
    moe = jnp.zeros(x_ref.shape, F32)
    for k in range(TOP_K):
        gate = jnp.sum(jnp.where(lane == ROUTE_G + k, rt, 0.0), axis=-1, keepdims=True)
        moe = moe + gate * _load_token_tiles(buf, (slot, k), tt, TOKEN_TILE)
    out = x_ref[...] + moe
    o_ref[...] = _rms(out, g_ref[...]) if final_norm else out


def _combine(y, dest, x2, route, final_g, final_norm):
    n, d = x2.shape
    tt = MOE_COMBINE_TILE
    n_steps = n // tt
    dest3 = dest.reshape(n_steps, 1, tt * TOP_K)
    kernel = functools.partial(_combine_kernel, final_norm=final_norm, n_steps=n_steps)
    dspec = lambda f: pl.BlockSpec((1, 1, tt * TOP_K), f, memory_space=pltpu.SMEM)
    return pl.pallas_call(
        kernel, grid=(n_steps,),
        in_specs=[dspec(lambda i: (i, 0, 0)),
                  dspec(lambda i: (jnp.minimum(i + 1, n_steps - 1), 0, 0)),
                  pl.BlockSpec(memory_space=pl.ANY),
                  pl.BlockSpec((tt, d), lambda i: (i, 0)),
                  pl.BlockSpec((tt, LANES), lambda i: (i, 0)),
                  pl.BlockSpec((1, d), lambda i: (0, 0))],
        out_specs=pl.BlockSpec((tt, d), lambda i: (i, 0)),
        out_shape=jax.ShapeDtypeStruct((n, d), F32),
        scratch_shapes=[pltpu.VMEM((2, TOP_K, tt * TOKEN_TILE, LANES), F32), pltpu.SemaphoreType.DMA((2,))],
        compiler_params=_cparams(1), name="moe_combine")(dest3, dest3, y, x2, route, final_g.reshape(1, d))


def _moe(h3, x2, route, cnt, w_gu, b_gu, w_dn, b_dn, final_g, final_norm):
    n, d = x2.shape
    assert d == TOKEN_TILE * LANES
    n_exp = w_gu.shape[0]
    e = route[:, ROUTE_E:ROUTE_E + TOP_K].astype(jnp.int32)
    rank = route[:, ROUTE_R:ROUTE_R + TOP_K].astype(jnp.int32)
    counts = cnt[0, :n_exp].astype(jnp.int32)
    padded = ((counts + MOE_BLOCK - 1) // MOE_BLOCK) * MOE_BLOCK
    pend = jnp.cumsum(padded)
    dest = (pend - padded)[e] + rank
    n_rows = n * TOP_K + n_exp * MOE_BLOCK
    n_blocks = n_rows // MOE_BLOCK
    starts = jnp.arange(n_blocks, dtype=jnp.int32) * MOE_BLOCK
    block_exp = jnp.minimum(jnp.sum((pend[None, :] <= starts[:, None]).astype(jnp.int32), axis=1), n_exp - 1)
    n_used = (pend[-1:] // MOE_BLOCK).astype(jnp.int32)
    xs = _dispatch(h3, dest, n_rows)
    y = _experts(xs, block_exp, n_used, w_gu, b_gu, w_dn, b_dn)
    return _combine(y, dest, x2, route, final_g, final_norm)


def _inproj_weights(w):
    gh = GDN_HEADS
    c = np.cumsum([0, gh * GDN_DK, gh * GDN_DK, gh * GDN_DV, gh * GDN_DV, gh, gh, NSA_HEADS * NSA_DK,
                   NSA_GROUPS * NSA_DK, NSA_GROUPS * NSA_DV, NSA_GROUPS * NSA_DK, NSA_GROUPS * NSA_DV,
                   NSA_GROUPS * NSA_DK, NSA_GROUPS * NSA_DV, NSA_HEADS * 3, D_MODEL, D_MODEL])
    n_small = 2 * gh + NSA_HEADS * 3
    small = jnp.concatenate([w[:, c[4]:c[6]], w[:, c[13]:c[14]], jnp.zeros((w.shape[0], LANES - n_small), w.dtype)],
                            axis=1)
    groups = [w[:, c[0]:c[3]],
              w[:, c[3]:c[4]],
              small,
              w[:, c[6]:c[7]],
              w[:, c[7]:c[8]],
              w[:, c[8]:c[9]],
              w[:, c[9]:c[13]],
              w[:, c[14]:c[16]]]
    return [g.astype(BF16) for g in groups]


GATE_COL0 = 2 * GDN_HEADS


def kernel(x, mem, attn_norm_g, w_in, gdn_conv_w, gdn_a_log, gdn_dt_bias, gdn_norm_g, cmp_pe_k, cmp_w1_k, cmp_w2_k, cmp_pe_v, cmp_w1_v, cmp_w2_v, w_branch_a, w_branch_b, w_mix_out, xattn_norm_g, mem_norm_g, xattn_w_q, xattn_w_kv, xattn_w_o, ffn_norm_g, router_w, router_b, w_gate_up, b_gate_up, w_down, b_down, final_norm_g):
    bsz, seq, d = x.shape
    n = bsz * seq
    depth = w_in.shape[0]
    x2d = x.reshape(n, d)
    for l in range(depth):
        gqkv, gz, small, nq, nkc, nvc, nkv, mab = _inproj(x2d, attn_norm_g[l], _inproj_weights(w_in[l]), [F32] * 8)
        b3 = lambda a: a.reshape(bsz, seq, a.shape[-1])
        o_a = _gdn(b3(gqkv), b3(gz), b3(small), gdn_conv_w[l], gdn_a_log[l], gdn_dt_bias[l], gdn_norm_g[l], bsz, seq)
        q_bf, qr_bf, kse, vsa, kw, vwa = _nsa_prep(nq, nkv, seq)
        kc = _compress(b3(nkc), cmp_pe_k[l], cmp_w1_k[l], cmp_w2_k[l])
        vc = _compress(b3(nvc), cmp_pe_v[l], cmp_w1_v[l], cmp_w2_v[l])
        oc, nind = _nsa_select(b3(q_bf), kc, vc, bsz, seq)
        o_b = _nsa_attention(b3(qr_bf), nind, oc, b3(kse), b3(vsa), b3(kw), b3(vwa), b3(small), bsz, seq, GATE_COL0)
        memkv = _mem_kv(mem, mem_norm_g[l], xattn_w_kv[l])
        x2, h3, route, cnt = _post_mixer(o_a.reshape(n, -1), o_b.reshape(n, -1), mab, x2d, w_branch_a[l],
                                         w_branch_b[l], w_mix_out[l], xattn_norm_g[l], xattn_w_q[l], memkv,
                                         xattn_w_o[l], ffn_norm_g[l], router_w[l], router_b[l], seq)
        x2d = _moe(h3, x2, route, cnt, w_gate_up[l], b_gate_up[l], w_down[l], b_down[l], final_norm_g,
                   final_norm=(l == depth - 1))
    return x2d.reshape(bsz, seq, d)
```

```python
import functools
import math

import jax
import jax.numpy as jnp
import numpy as np
from jax import lax
from jax.experimental import pallas as pl
from jax.experimental.pallas import tpu as pltpu

F32 = jnp.float32
BF16 = jnp.bfloat16
HIGHEST = lax.Precision.HIGHEST

D_MODEL = 1024
NORM_EPS = 1e-6
L2_EPS = 1e-6
ROPE_THETA = 500000.0
NEG_INF = -1e30

GDN_HEADS = 8
GDN_DK = 128
GDN_DV = 128
GDN_CONV = 4
GDN_CHUNK = 64

NSA_HEADS = 8
NSA_GROUPS = 2
NSA_REP = NSA_HEADS // NSA_GROUPS
NSA_DK = 128
NSA_DV = 128
ROT_DIM = NSA_DK // 4
CMP_LEN = 32
CMP_STRIDE = 16
SEL_LEN = 64
SEL_TOPK = 16
WINDOW = 512
Q_BLOCK = 128
SEL_FORCE = 1e4

MEM_LEN = 256
XATTN_HEADS = 4
XATTN_DH = 128

N_EXPERTS = 32
TOP_K = 4
SWIGLU_LIMIT = 7.0
SWIGLU_ALPHA = 1.702
MOE_BLOCK = 256

LANES = 128
VMEM_LIMIT = 56 * 1024 * 1024
SEL_TILE = 512


def _cparams(n_axes, vmem=VMEM_LIMIT):
    return pltpu.CompilerParams(dimension_semantics=("arbitrary",) * n_axes, vmem_limit_bytes=vmem)


def _dot(a, b, precision=None):
    return jnp.dot(a, b, preferred_element_type=F32, precision=precision)


def _dot_nt(a, b, precision=None):
    return lax.dot_general(a, b, (((1,), (1,)), ((), ())), preferred_element_type=F32, precision=precision)


def _sigmoid(x):
    return 1.0 / (1.0 + jnp.exp(-x))


def _iota(shape, dim):
    return lax.broadcasted_iota(jnp.int32, shape, dim)


def _inproj_kernel(x_ref, g_ref, c_ref, sa_ref, sb_ref, *refs, n_plain, rows_per_seq):
    w_plain, (w_nq, w_nkv) = refs[:n_plain], refs[n_plain:n_plain + 2]
    o_plain, nsa_out = refs[n_plain + 2:2 * n_plain + 2], refs[2 * n_plain + 2:]
    x = x_ref[...]
    ms = jnp.mean(x * x, axis=-1, keepdims=True)
    h = (x * lax.rsqrt(ms + NORM_EPS) * g_ref[...]).astype(BF16)
    for w_ref, o_ref in zip(w_plain, o_plain):
        o_ref[...] = _dot(h, w_ref[...]).astype(o_ref.dtype)
    t0 = (pl.program_id(0) % rows_per_seq) * x.shape[0]
    _nsa_prep_tile(_dot(h, w_nq[...]), _dot(h, w_nkv[...]), c_ref[...], sa_ref[...], sb_ref[...], t0, *nsa_out)


def _inproj(x2d, gain, plain_weights, w_nq, w_nkv, seq, tm=256):
    n, d = x2d.shape
    assert seq // SEL_LEN <= LANES
    g = NSA_GROUPS
    nt = seq // tm
    weights = list(plain_weights) + [w_nq, w_nkv]
    row = lambda w: pl.BlockSpec((tm, w), lambda i: (i, 0))
    tab = pl.BlockSpec((tm, LANES), lambda i: (i % nt, 0))
    nsa_w = [w_nq.shape[1], w_nq.shape[1], 2 * g * LANES, 2 * g * LANES, g * LANES, 2 * g * LANES]
    in_specs = [row(d), pl.BlockSpec((1, d), lambda i: (0, 0)), tab, tab, tab]
    in_specs += [pl.BlockSpec(w.shape, lambda i: (0, 0), pipeline_mode=pl.Buffered(1)) for w in weights]
    out_specs = [row(w.shape[1]) for w in plain_weights] + [row(w) for w in nsa_w]
    out_shape = ([jax.ShapeDtypeStruct((n, w.shape[1]), F32) for w in plain_weights]
                 + [jax.ShapeDtypeStruct((n, w), BF16) for w in nsa_w])
    outs = pl.pallas_call(
        functools.partial(_inproj_kernel, n_plain=len(plain_weights), rows_per_seq=nt), grid=(n // tm,),
        in_specs=in_specs, out_specs=out_specs, out_shape=out_shape,
        compiler_params=_cparams(1), name="inproj")(x2d, gain.reshape(1, d), *_rope_tables(seq), *weights)
    return outs[:len(plain_weights)], outs[len(plain_weights):]


def _rope_tables(seq):
    half = ROT_DIM // 2
    inv_freq = jnp.exp(-math.log(ROPE_THETA) * jnp.arange(half, dtype=F32) * (2.0 / ROT_DIM))
    ang = jnp.arange(seq).astype(F32)[:, None] * inv_freq[None, :]
    cos, sin = jnp.cos(ang), jnp.sin(ang)
    ones = jnp.ones((seq, LANES - ROT_DIM), F32)
    zeros = jnp.zeros((seq, LANES - ROT_DIM), F32)
    zh = jnp.zeros((seq, half), F32)
    c = jnp.concatenate([cos, cos, ones], axis=1)
    sa = jnp.concatenate([zh, sin, zeros], axis=1)
    sb = jnp.concatenate([-sin, zh, zeros], axis=1)
    return c, sa, sb


def _rope(xh, c, sa, sb):
    half = ROT_DIM // 2
    return xh * c + pltpu.roll(xh, half, 1) * sa + pltpu.roll(xh, LANES - half, 1) * sb


MASK_BIG = 2.0 ** 100
Q_PRESCALE = NSA_DK ** -0.5 * math.log2(math.e)


def _nsa_prep_tile(nq, nkv, c, sa, sb, t0, q_ref, qr_ref, kse_ref, vsa_ref, kw_ref, vwa_ref):
    tm = nq.shape[0]
    for h in range(NSA_HEADS):
        sl = slice(h * NSA_DK, (h + 1) * NSA_DK)
        xh = nq[:, sl] * Q_PRESCALE
        q_ref[:, sl] = xh.astype(BF16)
        qr_ref[:, sl] = _rope(xh, c, sa, sb).astype(BF16)
    key_blk = jnp.right_shift(t0 + _iota((tm, 1), 0), int(math.log2(SEL_LEN)))
    blk_neg = jnp.where(key_blk == _iota((1, LANES), 1), -MASK_BIG, 0.0).astype(BF16)
    ones = jnp.ones((tm, LANES), BF16)
    for g in range(NSA_GROUPS):
        col = lambda j: nkv[:, (j * NSA_GROUPS + g) * LANES:(j * NSA_GROUPS + g + 1) * LANES]
        wide = slice(2 * g * LANES, (2 * g + 1) * LANES)
        aux = slice((2 * g + 1) * LANES, (2 * g + 2) * LANES)
        kse_ref[:, wide] = _rope(col(0), c, sa, sb).astype(BF16)
        kse_ref[:, aux] = blk_neg
        vsa_ref[:, wide] = col(1).astype(BF16)
        vsa_ref[:, aux] = ones
        kw_ref[:, g * LANES:(g + 1) * LANES] = _rope(col(2), c, sa, sb).astype(BF16)
        vwa_ref[:, wide] = col(3).astype(BF16)
        vwa_ref[:, aux] = ones


def _compress_kernel(x_ref, pea_ref, peb_ref, w1a_ref, w1b_ref, w2_ref, o_ref):
    x = x_ref[0]
    y0 = _dot((x + pea_ref[...]).astype(BF16), w1a_ref[...])
    y1 = _dot((x + peb_ref[...]).astype(BF16), w1b_ref[...])
    n = y1.shape[0]
    pre = y0 + pltpu.roll(y1, n - 1, 0)
    act = pre * _sigmoid(pre)
    o_ref[0] = _dot(act.astype(BF16), w2_ref[...]).astype(o_ref.dtype)


def _compress(x, pe, w1, w2):
    bsz, seq, gd = x.shape
    g = NSA_GROUPS
    d = gd // g
    half = CMP_LEN // 2
    assert CMP_STRIDE == half
    xr = x.reshape(bsz, seq // half, half * gd)
    eye = jnp.eye(g, dtype=F32)

    def expand_w1(w):
        return jnp.einsum("ldk,gh->lgdhk", w.reshape(half, d, d), eye).reshape(half * gd, gd).astype(BF16)

    w1a, w1b = expand_w1(w1[: half * d]), expand_w1(w1[half * d:])
    w2b = jnp.einsum("dk,gh->gdhk", w2, eye).reshape(gd, gd).astype(BF16)
    pea = jnp.broadcast_to(pe[:half, None, :], (half, g, d)).reshape(1, half * gd)
    peb = jnp.broadcast_to(pe[half:, None, :], (half, g, d)).reshape(1, half * gd)
    nrow = seq // half
    full = lambda a: pl.BlockSpec(a.shape, lambda b: (0,) * a.ndim)
    return pl.pallas_call(
        _compress_kernel, grid=(bsz,),
        in_specs=[pl.BlockSpec((1, nrow, half * gd), lambda b: (b, 0, 0)),
                  full(pea), full(peb), full(w1a), full(w1b), full(w2b)],
        out_specs=pl.BlockSpec((1, nrow, gd), lambda b: (b, 0, 0)),
        out_shape=jax.ShapeDtypeStruct((bsz, nrow, gd), BF16),
        compiler_params=_cparams(1), name="nsa_compress")(xr, pea, peb, w1a, w1b, w2b)


def _sel_wmap(seq):
    n_cmp = (seq - CMP_LEN) // CMP_STRIDE + 1
    n_sel = seq // SEL_LEN
    r_s = SEL_LEN // CMP_STRIDE
    r_c = CMP_LEN // CMP_STRIDE
    w = np.zeros((seq // CMP_STRIDE, n_sel), np.float32)
    for s in range(n_sel):
        for m in range(r_s):
            for nn in range(r_c):
                c = s * r_s + m - nn
                if 0 <= c < n_cmp:
                    w[c, s] += 1.0
    return w


def _heads_to_rows(ref, r0):
    return jnp.concatenate([ref[0, pl.ds(r0, Q_BLOCK), r * NSA_DK:(r + 1) * NSA_DK] for r in range(NSA_REP)], axis=0)


def _split3(x):
    hi = x.astype(BF16)
    r1 = x - hi.astype(F32)
    mid = r1.astype(BF16)
    lo = (r1 - mid.astype(F32)).astype(BF16)
    return hi, mid, lo


SELECT_QBLOCKS = 8


def _nsa_select_kernel(q_ref, kc_ref, vc_ref, wmap_ref, oc_ref, nind_ref, imp_ref, *, seq):
    j = pl.program_id(2)
    rows = NSA_REP * Q_BLOCK
    n_sel = seq // SEL_LEN
    n_cmp_pad = seq // CMP_STRIDE
    n_q = SELECT_QBLOCKS * Q_BLOCK
    base = j * n_q
    kc = kc_ref[0]
    vc = vc_ref[0]
    cmp_end = _iota((1, n_cmp_pad), 1) * CMP_STRIDE + (CMP_LEN - 1)

    def per_block(qb_i, carry):
        r0 = pl.multiple_of(qb_i * Q_BLOCK, Q_BLOCK)
        tq1 = base + r0 + _iota((Q_BLOCK, 1), 0)
        s = _dot_nt(_heads_to_rows(q_ref, r0), kc).reshape(NSA_REP, Q_BLOCK, n_cmp_pad)
        mask = (cmp_end <= tq1)[None]
        s = jnp.where(mask, s, NEG_INF)
        p = jnp.where(mask, jnp.exp2(s - jnp.max(s, axis=-1, keepdims=True)), 0.0)
        denom = jnp.maximum(jnp.sum(p, axis=-1, keepdims=True), 1e-30)
        p_c = p / denom
        o_c = _dot(p_c.reshape(rows, n_cmp_pad).astype(BF16), vc)
        for r in range(NSA_REP):
            oc_ref[0, pl.ds(r0, Q_BLOCK), r * NSA_DV:(r + 1) * NSA_DV] = o_c[r * Q_BLOCK:(r + 1) * Q_BLOCK]
        p_sum = p_c[0]
        for r in range(1, NSA_REP):
            p_sum = p_sum + p_c[r]
        hi, mid, lo = _split3(p_sum)
        w = wmap_ref[...]
        imp_ref[pl.ds(r0, Q_BLOCK), :] = _dot(hi, w) + _dot(mid, w) + _dot(lo, w)
        return carry

    lax.fori_loop(0, SELECT_QBLOCKS, per_block, 0)

    n_pick = SEL_TOPK - 3
    tq = base + _iota((n_q, 1), 0)
    sel_ids = _iota((1, LANES), 1)
    cur = jnp.right_shift(tq, int(math.log2(SEL_LEN)))
    forced = (sel_ids == 0) | (sel_ids == cur) | (sel_ids == cur - 1)
    causal_blk = sel_ids * SEL_LEN <= tq
    cand = jnp.where(causal_blk & jnp.logical_not(forced), imp_ref[...], -jnp.inf)

    def write(picked):
        keep = forced | (picked > 0.5)
        nind_ref[0, 0] = jnp.where(causal_blk & keep, 0.0, 1.0).astype(BF16)

    score = cand
    picked = jnp.zeros((n_q, LANES), F32)
    for _ in range(n_pick):
        mx = jnp.max(score, axis=-1, keepdims=True)
        pick = (score == mx) & (mx > -jnp.inf)
        picked = jnp.where(pick, 1.0, picked)
        score = jnp.where(pick, -jnp.inf, score)
    write(picked)
    most = jnp.max(jnp.sum(picked, axis=-1, keepdims=True))

    @pl.when(most > n_pick)
    def _():
        lane = _iota((n_q, LANES), 1).astype(F32)
        score = cand
        picked = jnp.zeros((n_q, LANES), F32)
        for _ in range(n_pick):
            mx = jnp.max(score, axis=-1, keepdims=True)
            hit = (score == mx) & (mx > -jnp.inf)
            first = jnp.min(jnp.where(hit, lane, float(LANES)), axis=-1, keepdims=True)
            pick = lane == first
            picked = jnp.where(pick, 1.0, picked)
            score = jnp.where(pick, -jnp.inf, score)
        write(picked)


def _nsa_select(q, kc, vc, bsz, seq):
    g = NSA_GROUPS
    n_q = SELECT_QBLOCKS * Q_BLOCK
    wmap = jnp.asarray(_sel_wmap(seq), BF16)
    wmap = jnp.pad(wmap, ((0, 0), (0, LANES - wmap.shape[1])))
    cspec = pl.BlockSpec((1, seq // CMP_STRIDE, LANES), lambda b, gg, j: (b, 0, gg))
    kernel = functools.partial(_nsa_select_kernel, seq=seq)
    return pl.pallas_call(
        kernel, grid=(bsz, g, seq // n_q),
        in_specs=[pl.BlockSpec((1, n_q, NSA_REP * NSA_DK), lambda b, gg, j: (b, j, gg)), cspec, cspec,
                  pl.BlockSpec(wmap.shape, lambda b, gg, j: (0, 0))],
        out_specs=[pl.BlockSpec((1, n_q, NSA_REP * NSA_DV), lambda b, gg, j: (b, j, gg)),
                   pl.BlockSpec((1, 1, n_q, LANES), lambda b, gg, j: (b, gg, j, 0))],
        out_shape=[jax.ShapeDtypeStruct((bsz, seq, NSA_HEADS * NSA_DV), F32),
                   jax.ShapeDtypeStruct((bsz, g, seq, LANES), BF16)],
        scratch_shapes=[pltpu.VMEM((n_q, LANES), F32)],
        compiler_params=_cparams(3), name="nsa_select")(q, kc, vc, wmap)


def _nsa_attn_kernel(qr_ref, nind_ref, oc_ref, kse_ref, vsa_ref, kw_ref, vwa_ref, gate_ref, o_ref, m_ref, acc_ref,
                     sa_ref, sb_ref, *, gate_col0):
    g = pl.program_id(1)
    i = pl.program_id(2)
    q0 = i * Q_BLOCK
    rows = NSA_REP * Q_BLOCK
    tq1 = q0 + _iota((Q_BLOCK, 1), 0)

    qrb = _heads_to_rows(qr_ref, 0)
    lhs = jnp.concatenate([qrb, jnp.concatenate([nind_ref[0, 0]] * NSA_REP, axis=0)], axis=1)

    m_ref[...] = jnp.full(m_ref.shape, -MASK_BIG, F32)
    acc_ref[...] = jnp.zeros(acc_ref.shape, F32)

    n_tiles = q0 // SEL_TILE + 1

    def tile_start(t):
        return pl.multiple_of(jnp.minimum(t, n_tiles - 1) * SEL_TILE, SEL_TILE)

    def scores(t):
        return _dot_nt(lhs, kse_ref[0, pl.ds(tile_start(t), SEL_TILE), :])

    def tile_terms(s, t):
        k0 = tile_start(t)
        ok = (k0 + _iota((1, SEL_TILE), 1) <= tq1) & (t < n_tiles)
        bias = jnp.where(ok, 0.0, -MASK_BIG)
        s = (s.reshape(NSA_REP, Q_BLOCK, SEL_TILE) + bias[None]).reshape(rows, SEL_TILE)
        m_t = jnp.max(s, axis=-1, keepdims=True)
        p = jnp.exp2(s - m_t).astype(BF16)
        return m_t, _dot(p, vsa_ref[0, pl.ds(k0, SEL_TILE), :])

    def merge(terms):
        m_new = m_ref[...]
        for m_t, _ in terms:
            m_new = jnp.maximum(m_new, m_t)
        acc = jnp.exp2(m_ref[...] - m_new) * acc_ref[...]
        for m_t, pv in terms:
            acc = acc + jnp.exp2(m_t - m_new) * pv
        acc_ref[...] = acc
        m_ref[...] = m_new

    sa_ref[...] = scores(0)

    def tile_pair(p, carry):
        t0 = 2 * p
        sb_ref[...] = scores(t0 + 1)
        first = tile_terms(sa_ref[...], t0)
        sa_ref[...] = scores(t0 + 2)
        merge([first, tile_terms(sb_ref[...], t0 + 1)])
        return carry

    lax.fori_loop(0, (n_tiles + 1) // 2, tile_pair, 0)
    o_s = acc_ref[:, :NSA_DV] / acc_ref[:, NSA_DV:]

    wlen = WINDOW + Q_BLOCK
    w0 = pl.multiple_of(jnp.maximum(q0 - WINDOW, 0), Q_BLOCK)
    s_w = _dot_nt(qrb, kw_ref[0, pl.ds(w0, wlen), :]).reshape(NSA_REP, Q_BLOCK, wlen)
    dlt = tq1 - (w0 + _iota((1, wlen), 1))
    band = jnp.where((dlt >= 0) & (dlt < WINDOW), 0.0, -MASK_BIG)
    s_w = (s_w + band[None]).reshape(rows, wlen)
    p_w = jnp.exp2(s_w - jnp.max(s_w, axis=-1, keepdims=True)).astype(BF16)
    pv_w = _dot(p_w, vwa_ref[0, pl.ds(w0, wlen), :])
    o_w = pv_w[:, :NSA_DV] / pv_w[:, NSA_DV:]

    gts = _sigmoid(gate_ref[0])
    for r in range(NSA_REP):
        rs = slice(r * Q_BLOCK, (r + 1) * Q_BLOCK)
        cols = []
        for j in range(3):
            col = gate_col0 + (g * NSA_REP + r) * 3 + j
            onehot = _iota((1, LANES), 1) == col
            cols.append(jnp.sum(jnp.where(onehot, gts, 0.0), axis=-1, keepdims=True))
        o_c = oc_ref[0, :, r * NSA_DV:(r + 1) * NSA_DV]
        out = cols[0] * o_c + cols[1] * o_s[rs] + cols[2] * o_w[rs]
        o_ref[0, :, r * NSA_DV:(r + 1) * NSA_DV] = out.astype(o_ref.dtype)


def _nsa_attention(qr, nind, oc, kse, vsa, kw, vwa, small, bsz, seq, gate_col0):
    g = NSA_GROUPS
    rows = NSA_REP * Q_BLOCK
    qspec = pl.BlockSpec((1, Q_BLOCK, NSA_REP * NSA_DK), lambda b, gg, i: (b, i, gg))
    wide = pl.BlockSpec((1, seq, 2 * LANES), lambda b, gg, i: (b, 0, gg))
    kernel = functools.partial(_nsa_attn_kernel, gate_col0=gate_col0)
    return pl.pallas_call(
        kernel, grid=(bsz, g, seq // Q_BLOCK),
        in_specs=[qspec,
                  pl.BlockSpec((1, 1, Q_BLOCK, LANES), lambda b, gg, i: (b, gg, i, 0)),
                  qspec, wide, wide,
                  pl.BlockSpec((1, seq, LANES), lambda b, gg, i: (b, 0, gg)),
                  wide,
                  pl.BlockSpec((1, Q_BLOCK, LANES), lambda b, gg, i: (b, i, 0))],
        out_specs=qspec,
        out_shape=jax.ShapeDtypeStruct((bsz, seq, NSA_HEADS * NSA_DV), BF16),
        scratch_shapes=[pltpu.VMEM((rows, 1), F32), pltpu.VMEM((rows, 2 * NSA_DV), F32),
                        pltpu.VMEM((rows, SEL_TILE), F32), pltpu.VMEM((rows, SEL_TILE), F32)],
        compiler_params=_cparams(3), name="nsa_attn")(qr, nind, oc, kse, vsa, kw, vwa, small)


GDN_UNIT = 2 * GDN_CHUNK
GDN_STEP_UNITS = 2
CONV_HALO = 8


def _lcat(a, b):
    return jnp.concatenate([a, b], axis=1)


def _bdiag(a, b):
    za = jnp.zeros((a.shape[0], b.shape[1]), a.dtype)
    zb = jnp.zeros((b.shape[0], a.shape[1]), b.dtype)
    return jnp.concatenate([_lcat(a, za), _lcat(zb, b)], axis=0)


def _hilo(x):
    hi = x.astype(BF16)
    return hi, (x - hi.astype(F32)).astype(BF16)


def _rhs_of(hl):
    m = hl[0].shape[1] // 2
    return _bdiag(hl[0][:, :m], hl[0][:, m:]), _bdiag(hl[1][:, :m], hl[1][:, m:])


def _pair_rhs(x):
    return _rhs_of(_hilo(x))


def _mm3(lhs, rhs):
    return _dot(lhs[0], rhs[0]) + _dot(lhs[0], rhs[1]) + _dot(lhs[1], rhs[0])


def _pair_lower_inverse(mats, eye, b16, b32):
    eye_b = eye.astype(BF16)
    split = lambda xs: [_hilo(x) for x in xs]
    sq = lambda hls: [_mm3(h, _rhs_of(h)) for h in hls]
    plus = lambda h: (h[0] + eye_b, h[1])
    minus = lambda h: (eye_b - h[0], -h[1])
    d_hl = split([jnp.where(b16, a, 0.0) for a in mats])
    d2_hl = split(sq(d_hl))
    d4_hl = split(sq(d2_hl))
    d8_hl = split(sq(d4_hl))
    t = [_mm3(minus(a), _rhs_of(plus(b))) for a, b in zip(d_hl, d2_hl)]
    t = [_mm3(a, _rhs_of(plus(b))) for a, b in zip(split(t), d4_hl)]
    t = [_mm3(a, _rhs_of(plus(b))) for a, b in zip(split(t), d8_hl)]
    for pick in (lambda a: jnp.where(b32 & jnp.logical_not(b16), a, 0.0), lambda a: jnp.where(b32, 0.0, a)):
        t_hl = split(t)
        inner = [_mm3(_hilo(pick(a)), _rhs_of(h)) for a, h in zip(mats, t_hl)]
        t = [x - _mm3(h, _pair_rhs(y)) for x, h, y in zip(t, t_hl, inner)]
    return t


def _gdn_kernel(x_ref, halo_ref, z_ref, small_ref, convw_ref, alog_ref, dtb_ref, ng_ref, o_ref, s_ref):
    i = pl.program_id(1)
    u_len = GDN_UNIT
    c_len = GDN_CHUNK
    dk, dv = GDN_DK, GDN_DV
    hq = GDN_HEADS * dk

    @pl.when(i == 0)
    def _():
        s_ref[...] = jnp.zeros(s_ref.shape, F32)

    row = _iota((u_len, 2 * u_len), 0)
    col = _iota((u_len, 2 * u_len), 1) & (u_len - 1)
    same = jnp.right_shift(row, 6) == jnp.right_shift(col, 6)
    incl = same & (row >= col)
    strict = same & (row > col)
    b16 = jnp.right_shift(row, 4) == jnp.right_shift(col, 4)
    b32 = jnp.right_shift(row, 5) == jnp.right_shift(col, 5)
    eye = (row == col).astype(F32)
    eye1 = eye[:, :u_len] > 0.5
    rcol = _iota((u_len, 1), 0)
    lane = _iota((1, LANES), 1)
    first_rows = rcol < c_len
    first_cols = (_iota((1, 2 * u_len), 1) & (u_len - 1)) < c_len
    head0 = _iota((1, 2 * dv), 1) < dv
    halo_on = (i > 0).astype(F32)

    cum = incl[:, :u_len].astype(BF16)
    beta_all, gc_all = [], []
    for un in range(GDN_STEP_UNITS):
        sm = small_ref[0, un * u_len:(un + 1) * u_len, :]
        beta_all.append(_sigmoid(sm))
        xs = sm + dtb_ref[...]
        softplus = jnp.maximum(xs, 0.0) + jnp.log1p(jnp.exp(-jnp.abs(xs)))
        ld_hi, ld_mid, ld_lo = _split3(-jnp.exp(alog_ref[...]) * softplus)
        gc_all.append(_dot(cum, ld_hi) + _dot(cum, ld_mid) + _dot(cum, ld_lo))

    def pick_col(a, c):
        return jnp.sum(jnp.where(lane == c, a, 0.0), axis=-1, keepdims=True)

    def conv_silu(c0, un):
        r0 = un * u_len
        before = (halo_ref[0, :, c0:c0 + LANES] * halo_on if un == 0 else x_ref[0, r0 - CONV_HALO:r0, c0:c0 + LANES])
        xf = jnp.concatenate([before, x_ref[0, r0:r0 + u_len, c0:c0 + LANES]], axis=0)
        w = convw_ref[:, c0:c0 + LANES]
        y = w[0:1] * xf[CONV_HALO - 3:CONV_HALO - 3 + u_len]
        for j in range(1, GDN_CONV):
            off = CONV_HALO - (GDN_CONV - 1) + j
            y = y + w[j:j + 1] * xf[off:off + u_len]
        return y * _sigmoid(y)

    def head_terms(h, un):
        q = conv_silu(h * dk, un)
        k = conv_silu(hq + h * dk, un)
        v = conv_silu(2 * hq + h * dv, un)
        q = q * lax.rsqrt(jnp.sum(q * q, axis=-1, keepdims=True) + L2_EPS) * (dk ** -0.5)
        k = k * lax.rsqrt(jnp.sum(k * k, axis=-1, keepdims=True) + L2_EPS)
        beta = pick_col(beta_all[un], h)
        gcol = pick_col(gc_all[un], GDN_HEADS + h)
        grow = jnp.sum(jnp.where(eye1, gcol, 0.0), axis=0, keepdims=True)
        gl0 = jnp.sum(jnp.where(rcol == c_len - 1, gcol, 0.0), axis=0, keepdims=True)
        gl1 = jnp.sum(jnp.where(rcol == u_len - 1, gcol, 0.0), axis=0, keepdims=True)
        e_g = jnp.exp(gcol)
        kb = k * beta
        return dict(q=q, kb=kb, k_t=k.T, vb_kw=_lcat(v * beta, kb * e_g), qd=q * e_g,
                    diff=gcol - grow, kd_scale=jnp.exp(jnp.where(first_cols[:, :u_len], gl0, gl1) - grow),
                    g0=jnp.exp(gl0), g1=jnp.exp(gl1))

    n_pairs = GDN_HEADS // 2
    prs = range(n_pairs)

    def unit_local(un):
        pa = [head_terms(2 * p, un) for p in prs]
        pb = [head_terms(2 * p + 1, un) for p in prs]
        decay = [jnp.where(incl, jnp.exp(jnp.where(incl, _lcat(a["diff"], b["diff"]), 0.0)), 0.0)
                 for a, b in zip(pa, pb)]
        kt_rhs = [_pair_rhs(_lcat(a["k_t"], b["k_t"])) for a, b in zip(pa, pb)]
        gram = [_mm3(_hilo(_lcat(a["kb"], b["kb"])), r) for a, b, r in zip(pa, pb, kt_rhs)]
        t_mat = _pair_lower_inverse([jnp.where(strict, g_ * d_, 0.0) for g_, d_ in zip(gram, decay)], eye, b16, b32)
        vk = [(_hilo(a["vb_kw"]), _hilo(b["vb_kw"])) for a, b in zip(pa, pb)]
        uw = [_mm3(_hilo(t), (_bdiag(va[0], vb[0]), _bdiag(va[1], vb[1])))
              for t, (va, vb) in zip(t_mat, vk)]
        u = [_lcat(x[:, :dv], x[:, dv + dk:2 * dv + dk]) for x in uw]
        w = [_lcat(x[:, dv:dv + dk], x[:, 2 * dv + dk:]).astype(BF16) for x in uw]
        qk = [jnp.where(incl, _dot(_lcat(a["q"], b["q"]).astype(BF16), r[0]) * d_, 0.0).astype(BF16)
              for a, b, r, d_ in zip(pa, pb, kt_rhs, decay)]
        w_qd = [jnp.concatenate([w_, _lcat(a["qd"], b["qd"]).astype(BF16)], axis=0)
                for w_, a, b in zip(w, pa, pb)]
        kd = [_lcat(a["k_t"] * a["kd_scale"], b["k_t"] * b["kd_scale"]) for a, b in zip(pa, pb)]
        qk_kd = [[jnp.concatenate([qk_, jnp.where(first_cols, x, 0.0).astype(BF16)], axis=0) for qk_, x in zip(qk, kd)],
                 [jnp.concatenate([qk_, jnp.where(first_cols, 0.0, x).astype(BF16)], axis=0) for qk_, x in zip(qk, kd)]]
        gains = [[jnp.where(head0, a[key], b[key]) for a, b in zip(pa, pb)] for key in ("g0", "g1")]
        return u, w_qd, qk_kd, gains

    local = [unit_local(un) for un in range(GDN_STEP_UNITS)]
    s = [s_ref[p] for p in prs]
    for un in range(GDN_STEP_UNITS):
        u, w_qd, qk_kd, gains = local[un]
        outs = []
        for c in range(2):
            sb = [x.astype(BF16) for x in s]
            r1 = [_dot(w_qd[p], _bdiag(sb[p][:, :dv], sb[p][:, dv:])) for p in prs]
            vn = [(u[p] - r1[p][:u_len]).astype(BF16) for p in prs]
            r2 = [_dot(qk_kd[c][p], _bdiag(vn[p][:, :dv], vn[p][:, dv:])) for p in prs]
            outs.append([r1[p][u_len:] + r2[p][:u_len] for p in prs])
            s = [s[p] * gains[c][p] + r2[p][u_len:] for p in prs]
        for p in prs:
            o_pair = jnp.where(first_rows, outs[0][p], outs[1][p])
            for j in range(2):
                h = 2 * p + j
                rows = slice(un * u_len, (un + 1) * u_len)
                o = o_pair[:, j * dv:(j + 1) * dv]
                o = o * lax.rsqrt(jnp.mean(o * o, axis=-1, keepdims=True) + NORM_EPS) * ng_ref[...]
                z = z_ref[0, rows, h * dv:(h + 1) * dv]
                o_ref[0, rows, h * dv:(h + 1) * dv] = (o * (z * _sigmoid(z))).astype(o_ref.dtype)
    for p in prs:
        s_ref[p] = s[p]


def _gdn(gqkv, gz, small, conv_w, a_log, dt_bias, norm_g, bsz, seq):
    u_len = GDN_UNIT * GDN_STEP_UNITS
    c_all = gqkv.shape[-1]
    hv = GDN_HEADS * GDN_DV
    pad = jnp.zeros((GDN_HEADS,), F32)
    tail = jnp.zeros((LANES - 2 * GDN_HEADS,), F32)
    alog_row = jnp.concatenate([pad, a_log.astype(F32), tail]).reshape(1, LANES)
    dtb_row = jnp.concatenate([pad, dt_bias.astype(F32), tail]).reshape(1, LANES)
    per_halo = u_len // CONV_HALO
    return pl.pallas_call(
        _gdn_kernel, grid=(bsz, seq // u_len),
        in_specs=[pl.BlockSpec((1, u_len, c_all), lambda b, i: (b, i, 0)),
                  pl.BlockSpec((1, CONV_HALO, c_all), lambda b, i: (b, jnp.maximum(i * per_halo - 1, 0), 0)),
                  pl.BlockSpec((1, u_len, hv), lambda b, i: (b, i, 0)),
                  pl.BlockSpec((1, u_len, LANES), lambda b, i: (b, i, 0)),
                  pl.BlockSpec(conv_w.shape, lambda b, i: (0, 0)),
                  pl.BlockSpec((1, LANES), lambda b, i: (0, 0)),
                  pl.BlockSpec((1, LANES), lambda b, i: (0, 0)),
                  pl.BlockSpec((1, GDN_DV), lambda b, i: (0, 0))],
        out_specs=pl.BlockSpec((1, u_len, hv), lambda b, i: (b, i, 0)),
        out_shape=jax.ShapeDtypeStruct((bsz, seq, hv), BF16),
        scratch_shapes=[pltpu.VMEM((GDN_HEADS // 2, GDN_DK, 2 * GDN_DV), F32)],
        compiler_params=_cparams(2), name="gdn")(gqkv, gqkv, gz, small, conv_w, alog_row, dtb_row,
                                                  norm_g.reshape(1, GDN_DV))


def _rms(x, gain):
    return x * lax.rsqrt(jnp.mean(x * x, axis=-1, keepdims=True) + NORM_EPS) * gain


def _mem_kv_kernel(m_ref, g_ref, w_ref, o_ref):
    h = _rms(m_ref[0], g_ref[...]).astype(BF16)
    o_ref[0] = _dot(h, w_ref[...]).astype(o_ref.dtype)


def _mem_kv(mem, gain, w_kv):
    bsz, m_len, d = mem.shape
    w = w_kv.astype(BF16)
    return pl.pallas_call(
        _mem_kv_kernel, grid=(bsz,),
        in_specs=[pl.BlockSpec((1, m_len, d), lambda b: (b, 0, 0)), pl.BlockSpec((1, d), lambda b: (0, 0)),
                  pl.BlockSpec(w.shape, lambda b: (0, 0))],
        out_specs=pl.BlockSpec((1, m_len, w.shape[1]), lambda b: (b, 0, 0)),
        out_shape=jax.ShapeDtypeStruct((bsz, m_len, w.shape[1]), BF16),
        compiler_params=_cparams(1), name="mem_kv")(mem, gain.reshape(1, d), w)


ROUTE_E, ROUTE_G, ROUTE_R = 0, TOP_K, 2 * TOP_K


def _store_token_tiles(ref, row0, x):
    m, d = x.shape
    g = d // LANES
    for j in range(g):
        ref[pl.ds(row0 * g + j, m, stride=g), :] = x[:, j * LANES:(j + 1) * LANES]


def _load_token_tiles(ref, idx, m, g):
    return jnp.concatenate([ref[idx + (pl.ds(j, m, stride=g), slice(None))] for j in range(g)], axis=1)


def _post_mixer_kernel(oa_ref, ob_ref, mab_ref, x_ref, wa_ref, wb_ref, wmix_ref, gx_ref, wq_ref, km_ref, vm_ref,
                       wo_ref, gf_ref, rw_ref, rb_ref, x2_ref, h3_ref, route_ref, cnt_ref, run_ref):
    i = pl.program_id(0)
    tm, d = x_ref.shape
    ts = POST_SUB
    subs = [slice(j * ts, (j + 1) * ts) for j in range(tm // ts)]

    @pl.when(i == 0)
    def _():
        run_ref[...] = jnp.zeros(run_ref.shape, F32)

    ya = [_dot(oa_ref[r, :], wa_ref[...]) for r in subs]
    yb = [_dot(ob_ref[r, :], wb_ref[...]) for r in subs]
    mixed = [(_sigmoid(mab_ref[r, :d]) * a + _sigmoid(mab_ref[r, d:]) * b).astype(BF16) for r, a, b in zip(subs, ya, yb)]
    x1 = [x_ref[r, :] + _dot(m, wmix_ref[...]) for r, m in zip(subs, mixed)]

    q = [_dot(_rms(x, gx_ref[...]).astype(BF16), wq_ref[...]).astype(BF16) for x in x1]
    heads = []
    for hd in range(XATTN_HEADS):
        sl = slice(hd * XATTN_DH, (hd + 1) * XATTN_DH)
        s = [_dot_nt(qq[:, sl], km_ref[0, :, sl]) * (XATTN_DH ** -0.5) for qq in q]
        p = [jnp.exp(x - jnp.max(x, axis=-1, keepdims=True)) for x in s]
        p = [(x / jnp.sum(x, axis=-1, keepdims=True)).astype(BF16) for x in p]
        heads.append([_dot(x, vm_ref[0, :, sl]) for x in p])
    o = [jnp.concatenate([heads[hd][j] for hd in range(XATTN_HEADS)], axis=-1).astype(BF16) for j in range(len(subs))]
    x2 = [x + _dot(oo, wo_ref[...]) for x, oo in zip(x1, o)]

    h3 = [_rms(x, gf_ref[...]) for x in x2]
    logits = []
    for h in h3:
        acc = rb_ref[...]
        for term in _split3(h):
            both = _dot(term, rw_ref[...])
            acc = acc + (both[:, :LANES] + both[:, LANES:])
        logits.append(acc)
    for r, x, h in zip(subs, x2, h3):
        x2_ref[r, :] = x
        _store_token_tiles(h3_ref, r.start, h)

    lane = _iota((ts, LANES), 1).astype(F32)
    work = logits
    vals, idxs = [], []
    for _ in range(TOP_K):
        mx = [jnp.max(w, axis=-1, keepdims=True) for w in work]
        first = [jnp.min(jnp.where(w == m, lane, float(LANES)), axis=-1, keepdims=True) for w, m in zip(work, mx)]
        vals.append(mx)
        idxs.append(first)
        work = [jnp.where(lane == f, -jnp.inf, w) for w, f in zip(work, first)]
    earlier = (_iota((ts, ts), 0) > _iota((ts, ts), 1)).astype(BF16)
    run = run_ref[...]
    for j, r in enumerate(subs):
        exps = [jnp.exp(vals[k][j] - vals[0][j]) for k in range(TOP_K)]
        den = exps[0]
        for e in exps[1:]:
            den = den + e
        onehot = jnp.zeros((ts, LANES), F32)
        for k in range(TOP_K):
            onehot = onehot + (lane == idxs[k][j]).astype(F32)
        before = run + _dot(earlier, onehot.astype(BF16))
        route = jnp.zeros((ts, LANES), F32)
        for k in range(TOP_K):
            rank = jnp.sum(jnp.where(lane == idxs[k][j], before, 0.0), axis=-1, keepdims=True)
            route = jnp.where(lane == float(ROUTE_E + k), idxs[k][j], route)
            route = jnp.where(lane == float(ROUTE_G + k), exps[k] / den, route)
            route = jnp.where(lane == float(ROUTE_R + k), rank, route)
        route_ref[r, :] = route
        run = run + jnp.sum(onehot, axis=0, keepdims=True)
    run_ref[...] = run
    cnt_ref[...] = run


POST_SUB = 256


def _post_mixer(o_a, o_b, mab, x2d, w_a, w_b, w_mix, gx, w_q, memkv, w_o, gf, router_w, router_b, seq, tm=512):
    n, d = x2d.shape
    xd = w_q.shape[1]
    m_len = memkv.shape[1]
    per_b = seq // tm
    n_exp = router_w.shape[1]
    rw32 = jnp.concatenate([router_w.astype(F32), jnp.zeros((d, LANES - n_exp), F32)], axis=1)
    rw_hi = rw32.astype(BF16)
    rw = jnp.concatenate([rw_hi, (rw32 - rw_hi.astype(F32)).astype(BF16)], axis=1)
    rb = jnp.concatenate([router_b.astype(F32), jnp.full((LANES - n_exp,), NEG_INF, F32)]).reshape(1, LANES)
    row = lambda w: pl.BlockSpec((tm, w), lambda i: (i, 0))
    full = lambda a: pl.BlockSpec(a.shape, lambda i: (0,) * a.ndim)
    wa, wb, wm, wq, wo = (w.astype(BF16) for w in (w_a, w_b, w_mix, w_q, w_o))
    gx2, gf2 = gx.reshape(1, d), gf.reshape(1, d)
    return pl.pallas_call(
        _post_mixer_kernel, grid=(n // tm,),
        in_specs=[row(d), row(d), row(2 * d), row(d), full(wa), full(wb), full(wm), full(gx2), full(wq),
                  pl.BlockSpec((1, m_len, xd), lambda i: (i // per_b, 0, 0)),
                  pl.BlockSpec((1, m_len, xd), lambda i: (i // per_b, 0, 1)),
                  full(wo), full(gf2), full(rw), full(rb)],
        out_specs=[row(d), pl.BlockSpec((tm * (d // LANES), LANES), lambda i: (i, 0)), row(LANES),
                   pl.BlockSpec((1, LANES), lambda i: (0, 0))],
        out_shape=[jax.ShapeDtypeStruct((n, d), F32), jax.ShapeDtypeStruct((n * (d // LANES), LANES), F32),
                   jax.ShapeDtypeStruct((n, LANES), F32), jax.ShapeDtypeStruct((1, LANES), F32)],
        scratch_shapes=[pltpu.VMEM((1, LANES), F32)],
        compiler_params=_cparams(1), name="post_mixer")(o_a, o_b, mab, x2d, wa, wb, wm, gx2, wq, memkv, memkv, wo,
                                                        gf2, rw, rb)


MOE_DISPATCH_TILE = 512
MOE_COMBINE_TILE = 256
DMA_LOOP_UNROLL = 4


TOKEN_TILE = D_MODEL // LANES


def _token_rows(t):
    return pl.ds(pl.multiple_of(t * TOKEN_TILE, TOKEN_TILE), TOKEN_TILE)


def _dispatch_kernel(dest_ref, pads_ref, h_ref, xs_ref, zero_ref, sem, zsem):
    first = pl.program_id(0) == 0
    tt = h_ref.shape[0] // TOKEN_TILE

    def pad_copy(row):
        return pltpu.make_async_copy(zero_ref, xs_ref.at[_token_rows(row)], zsem)

    def for_each_pad_row(fn):
        def per_range(r, c):
            def per_row(j, c2):
                fn(pads_ref[0, r] + j)
                return c2
            lax.fori_loop(0, pads_ref[1, r], per_row, 0)
            return c
        lax.fori_loop(0, pads_ref.shape[1], per_range, 0)

    @pl.when(first)
    def _():
        zero_ref[...] = jnp.zeros(zero_ref.shape, zero_ref.dtype)
        for_each_pad_row(lambda row: pad_copy(row).start())

    def row_copy(t, dst_row):
        return pltpu.make_async_copy(h_ref.at[_token_rows(t)], xs_ref.at[_token_rows(dst_row)], sem)

    def start(t, c):
        for k in range(TOP_K):
            row_copy(t, dest_ref[0, 0, t * TOP_K + k]).start(priority=k % 2)
        return c

    def wait(t, c):
        for k in range(TOP_K):
            row_copy(t, dest_ref[0, 0, t * TOP_K + k]).wait()
        return c

    lax.fori_loop(0, tt, start, 0, unroll=DMA_LOOP_UNROLL)
    lax.fori_loop(0, tt, wait, 0, unroll=DMA_LOOP_UNROLL)

    @pl.when(first)
    def _():
        for_each_pad_row(lambda row: pad_copy(row).wait())


def _dispatch(h3, dest, pads, n_rows):
    n = h3.shape[0] // TOKEN_TILE
    tt = MOE_DISPATCH_TILE
    dest3 = dest.reshape(n // tt, 1, tt * TOP_K)
    return pl.pallas_call(
        _dispatch_kernel, grid=(n // tt,),
        in_specs=[pl.BlockSpec((1, 1, tt * TOP_K), lambda i: (i, 0, 0), memory_space=pltpu.SMEM),
                  pl.BlockSpec(memory_space=pltpu.SMEM),
                  pl.BlockSpec((tt * TOKEN_TILE, LANES), lambda i: (i, 0))],
        out_specs=pl.BlockSpec(memory_space=pl.ANY),
        out_shape=jax.ShapeDtypeStruct((n_rows * TOKEN_TILE, LANES), h3.dtype),
        scratch_shapes=[pltpu.VMEM((TOKEN_TILE, LANES), h3.dtype), pltpu.SemaphoreType.DMA(()),
                        pltpu.SemaphoreType.DMA(())],
        compiler_params=_cparams(1), name="moe_dispatch")(dest3, pads, h3)


def _expert_kernel(be_ref, nu_ref, xs_ref, wgu_ref, bgu_ref, wdn_ref, bdn_ref, y_ref, wgu_bf, wdn_bf):
    j = pl.program_id(0)
    d_exp = wdn_ref.shape[1]
    used = j < nu_ref[0]
    fresh = (j == 0) | (be_ref[j] != be_ref[jnp.maximum(j - 1, 0)])
    chunk = 128

    @pl.when(used & fresh)
    def _():
        def cast_gu(c, carry):
            r0 = pl.multiple_of(c * chunk, chunk)
            wgu_bf[pl.ds(r0, chunk), :] = wgu_ref[0, pl.ds(r0, chunk), :].astype(BF16)
            return carry

        def cast_dn(c, carry):
            r0 = pl.multiple_of(c * chunk, chunk)
            wdn_bf[pl.ds(r0, chunk), :] = wdn_ref[0, pl.ds(r0, chunk), :].astype(BF16)
            return carry

        lax.fori_loop(0, wgu_bf.shape[0] // chunk, cast_gu, 0)
        lax.fori_loop(0, wdn_bf.shape[0] // chunk, cast_dn, 0)

    @pl.when(used)
    def _():
        x = _load_token_tiles(xs_ref, (), MOE_BLOCK, TOKEN_TILE).astype(BF16)
        gu = _dot(x, wgu_bf[...]) + bgu_ref[0]
        gate = jnp.minimum(gu[:, :d_exp], SWIGLU_LIMIT)
        up = jnp.clip(gu[:, d_exp:], -SWIGLU_LIMIT, SWIGLU_LIMIT)
        act = gate * _sigmoid(gate * SWIGLU_ALPHA) * (up + 1.0)
        _store_token_tiles(y_ref, 0, _dot(act.astype(BF16), wdn_bf[...]) + bdn_ref[0])

    @pl.when(jnp.logical_not(used))
    def _():
        y_ref[...] = jnp.zeros(y_ref.shape, y_ref.dtype)


def _experts(xs, block_exp, n_used, w_gu, b_gu, w_dn, b_dn):
    n_exp, d, gu_w = w_gu.shape
    n_rows = xs.shape[0] // TOKEN_TILE
    d_exp = w_dn.shape[1]
    n_blocks = n_rows // MOE_BLOCK
    tiles = pl.BlockSpec((MOE_BLOCK * TOKEN_TILE, LANES), lambda j, be, nu: (j, 0))
    grid_spec = pltpu.PrefetchScalarGridSpec(
        num_scalar_prefetch=2, grid=(n_blocks,),
        in_specs=[tiles,
                  pl.BlockSpec((1, d, gu_w), lambda j, be, nu: (be[j], 0, 0)),
                  pl.BlockSpec((1, 1, gu_w), lambda j, be, nu: (be[j], 0, 0)),
                  pl.BlockSpec((1, d_exp, d), lambda j, be, nu: (be[j], 0, 0)),
                  pl.BlockSpec((1, 1, d), lambda j, be, nu: (be[j], 0, 0))],
        out_specs=tiles,
        scratch_shapes=[pltpu.VMEM((d, gu_w), BF16), pltpu.VMEM((d_exp, d), BF16)])
    return pl.pallas_call(
        _expert_kernel, grid_spec=grid_spec,
        out_shape=jax.ShapeDtypeStruct(xs.shape, F32),
        compiler_params=_cparams(1), name="moe_experts")(
            block_exp, n_used, xs, w_gu, b_gu.reshape(n_exp, 1, gu_w), w_dn, b_dn.reshape(n_exp, 1, d))


def _combine_kernel(dest_ref, dest_next_ref, y_ref, x_ref, route_ref, g_ref, o_ref, buf, sem, *, final_norm, n_steps):
    i = pl.program_id(0)
    tt = x_ref.shape[0]
    slot = i % 2

    def row_copy(dref, into, t, k):
        return pltpu.make_async_copy(y_ref.at[_token_rows(dref[0, 0, t * TOP_K + k])], buf.at[into, k, _token_rows(t)],
                                     sem.at[into])

    def gather(dref, into):
        def start(t, c):
            for k in range(TOP_K):
                row_copy(dref, into, t, k).start(priority=k % 2)
            return c
        lax.fori_loop(0, tt, start, 0, unroll=DMA_LOOP_UNROLL)

    @pl.when(i == 0)
    def _():
        gather(dest_ref, 0)

    @pl.when(i + 1 < n_steps)
    def _():
        gather(dest_next_ref, 1 - slot)

    def wait(t, c):
        for k in range(TOP_K):
            row_copy(dest_ref, slot, t, k).wait()
        return c

    lax.fori_loop(0, tt, wait, 0, unroll=DMA_LOOP_UNROLL)
    lane = _iota((1, LANES), 1)
    rt = route_ref[...]
    moe = jnp.zeros(x_ref.shape, F32)
    for k in range(TOP_K):
        gate = jnp.sum(jnp.where(lane == ROUTE_G + k, rt, 0.0), axis=-1, keepdims=True)
        moe = moe + gate * _load_token_tiles(buf, (slot, k), tt, TOKEN_TILE)
    out = x_ref[...] + moe
    o_ref[...] = _rms(out, g_ref[...]) if final_norm else out


def _combine(y, dest, x2, route, final_g, final_norm):
    n, d = x2.shape
    tt = MOE_COMBINE_TILE
    n_steps = n // tt
    dest3 = dest.reshape(n_steps, 1, tt * TOP_K)
    kernel = functools.partial(_combine_kernel, final_norm=final_norm, n_steps=n_steps)
    dspec = lambda f: pl.BlockSpec((1, 1, tt * TOP_K), f, memory_space=pltpu.SMEM)
    return pl.pallas_call(
        kernel, grid=(n_steps,),
        in_specs=[dspec(lambda i: (i, 0, 0)),
                  dspec(lambda i: (jnp.minimum(i + 1, n_steps - 1), 0, 0)),
                  pl.BlockSpec(memory_space=pl.ANY),
                  pl.BlockSpec((tt, d), lambda i: (i, 0)),
                  pl.BlockSpec((tt, LANES), lambda i: (i, 0)),
                  pl.BlockSpec((1, d), lambda i: (0, 0))],
        out_specs=pl.BlockSpec((tt, d), lambda i: (i, 0)),
        out_shape=jax.ShapeDtypeStruct((n, d), F32),
        scratch_shapes=[pltpu.VMEM((2, TOP_K, tt * TOKEN_TILE, LANES), F32), pltpu.SemaphoreType.DMA((2,))],
        compiler_params=_cparams(1), name="moe_combine")(dest3, dest3, y, x2, route, final_g.reshape(1, d))


def _moe(h3, x2, route, cnt, w_gu, b_gu, w_dn, b_dn, final_g, final_norm):
    n, d = x2.shape
    assert d == TOKEN_TILE * LANES
    n_exp = w_gu.shape[0]
    e = route[:, ROUTE_E:ROUTE_E + TOP_K].astype(jnp.int32)
    rank = route[:, ROUTE_R:ROUTE_R + TOP_K].astype(jnp.int32)
    counts = cnt[0, :n_exp].astype(jnp.int32)
    padded = ((counts + MOE_BLOCK - 1) // MOE_BLOCK) * MOE_BLOCK
    pend = jnp.cumsum(padded)
    dest = (pend - padded)[e] + rank
    n_rows = n * TOP_K + n_exp * MOE_BLOCK
    n_blocks = n_rows // MOE_BLOCK
    starts = jnp.arange(n_blocks, dtype=jnp.int32) * MOE_BLOCK
    block_exp = jnp.minimum(jnp.sum((pend[None, :] <= starts[:, None]).astype(jnp.int32), axis=1), n_exp - 1)
    n_used = (pend[-1:] // MOE_BLOCK).astype(jnp.int32)
    pads = jnp.stack([jnp.concatenate([pend - padded + counts, pend[-1:]]),
                      jnp.concatenate([padded - counts, n_rows - pend[-1:]])]).astype(jnp.int32)
    xs = _dispatch(h3, dest, pads, n_rows)
    y = _experts(xs, block_exp, n_used, w_gu, b_gu, w_dn, b_dn)
    return _combine(y, dest, x2, route, final_g, final_norm)


def _inproj_weights(w):
    gh = GDN_HEADS
    c = np.cumsum([0, gh * GDN_DK, gh * GDN_DK, gh * GDN_DV, gh * GDN_DV, gh, gh, NSA_HEADS * NSA_DK,
                   NSA_GROUPS * NSA_DK, NSA_GROUPS * NSA_DV, NSA_GROUPS * NSA_DK, NSA_GROUPS * NSA_DV,
                   NSA_GROUPS * NSA_DK, NSA_GROUPS * NSA_DV, NSA_HEADS * 3, D_MODEL, D_MODEL])
    n_small = 2 * gh + NSA_HEADS * 3
    small = jnp.concatenate([w[:, c[4]:c[6]], w[:, c[13]:c[14]], jnp.zeros((w.shape[0], LANES - n_small), w.dtype)],
                            axis=1)
    plain = [w[:, c[0]:c[3]],
             w[:, c[3]:c[4]],
             small,
             w[:, c[7]:c[8]],
             w[:, c[8]:c[9]],
             w[:, c[14]:c[16]]]
    w_nq = w[:, c[6]:c[7]]
    w_nkv = w[:, c[9]:c[13]]
    return [g.astype(BF16) for g in plain], w_nq.astype(BF16), w_nkv.astype(BF16)


GATE_COL0 = 2 * GDN_HEADS


def kernel(x, mem, attn_norm_g, w_in, gdn_conv_w, gdn_a_log, gdn_dt_bias, gdn_norm_g, cmp_pe_k, cmp_w1_k, cmp_w2_k, cmp_pe_v, cmp_w1_v, cmp_w2_v, w_branch_a, w_branch_b, w_mix_out, xattn_norm_g, mem_norm_g, xattn_w_q, xattn_w_kv, xattn_w_o, ffn_norm_g, router_w, router_b, w_gate_up, b_gate_up, w_down, b_down, final_norm_g):
    bsz, seq, d = x.shape
    n = bsz * seq
    depth = w_in.shape[0]
    x2d = x.reshape(n, d)
    for l in range(depth):
        (gqkv, gz, small, nkc, nvc, mab), (q_bf, qr_bf, kse, vsa, kw, vwa) = _inproj(
            x2d, attn_norm_g[l], *_inproj_weights(w_in[l]), seq)
        b3 = lambda a: a.reshape(bsz, seq, a.shape[-1])
        o_a = _gdn(b3(gqkv), b3(gz), b3(small), gdn_conv_w[l], gdn_a_log[l], gdn_dt_bias[l], gdn_norm_g[l], bsz, seq)
        kc = _compress(b3(nkc), cmp_pe_k[l], cmp_w1_k[l], cmp_w2_k[l])
        vc = _compress(b3(nvc), cmp_pe_v[l], cmp_w1_v[l], cmp_w2_v[l])
        oc, nind = _nsa_select(b3(q_bf), kc, vc, bsz, seq)
        o_b = _nsa_attention(b3(qr_bf), nind, oc, b3(kse), b3(vsa), b3(kw), b3(vwa), b3(small), bsz, seq, GATE_COL0)
        memkv = _mem_kv(mem, mem_norm_g[l], xattn_w_kv[l])
        x2, h3, route, cnt = _post_mixer(o_a.reshape(n, -1), o_b.reshape(n, -1), mab, x2d, w_branch_a[l],
                                         w_branch_b[l], w_mix_out[l], xattn_norm_g[l], xattn_w_q[l], memkv,
                                         xattn_w_o[l], ffn_norm_g[l], router_w[l], router_b[l], seq)
        x2d = _moe(h3, x2, route, cnt, w_gate_up[l], b_gate_up[l], w_down[l], b_down[l], final_norm_g,
                   final_norm=(l == depth - 1))
    return x2d.reshape(bsz, seq, d)
```
